```python
import math
import jax
import jax.numpy as jnp
from jax import lax
import numpy as np

D_MODEL = 2048
BATCH = 32
SEQ = 256
DEPTH = 4
DEC_BATCH = 4
DEC_SEQ = 2048
PAST_LEN = 256

GRID_W = 64
GROUP_WIDTH = D_MODEL // 4
A_V = 128
A_HEADS = GROUP_WIDTH // A_V
A_QK = A_V // 2
B_GD = 128
B_GROUPS = GROUP_WIDTH // B_GD
C_HD = 128
C_HEADS = GROUP_WIDTH // C_HD
C_KV_HEADS = C_HEADS // 2
C_WINDOW = 128
D_HD = 128
D_HEADS = GROUP_WIDTH // D_HD
CHUNK = 128
Q_BLOCK = 128
IN_SIZES = (A_HEADS * A_QK, A_HEADS * A_QK, A_HEADS * A_QK, A_HEADS * A_QK, A_HEADS * A_V,
            GROUP_WIDTH,
            C_HEADS * C_HD, C_KV_HEADS * C_HD, C_KV_HEADS * C_HD,
            GROUP_WIDTH, GROUP_WIDTH)
IN_WIDTH = sum(IN_SIZES)
IN_SPLITS = tuple(int(s) for s in np.cumsum(IN_SIZES)[:-1])
D_FF = 5632
N_EXPERTS = 8
TOP_K = 2
D_FF_EXPERT = 2816
N_DENSE = (DEPTH + 1) // 2
N_MOE = DEPTH // 2
ROPE_BASE = 10000.0
LN_EPS = 1e-6
SUBLN_EPS = 1e-5
ALPHA = (2.0 * DEPTH) ** 0.25
BETA = (8.0 * DEPTH) ** -0.25
NEG = -1e30

kernel_name = "hybrid_diffusion_prefix_trunk_step"


def layer_norm(x):
    xf = x.astype(jnp.float32)
    xc = xf - jnp.mean(xf, -1, keepdims=True)
    return (xc * lax.rsqrt(jnp.mean(xc * xc, -1, keepdims=True) + LN_EPS)).astype(x.dtype)


def layer_norm_affine(x, g, b):
    return layer_norm(x) * g + b


def rms_norm(x, g):
    xf = x.astype(jnp.float32)
    return (xf * lax.rsqrt(jnp.mean(xf * xf, -1, keepdims=True) + SUBLN_EPS)).astype(x.dtype) * g


def axial_rope(n_tokens, head_dim):
    rows = n_tokens // GRID_W
    row = jnp.repeat(jnp.arange(rows, dtype=jnp.float32), GRID_W)
    col = jnp.tile(jnp.arange(GRID_W, dtype=jnp.float32), rows)
    n_freq = head_dim // 4
    inv = jnp.power(ROPE_BASE, -jnp.arange(n_freq, dtype=jnp.float32) / n_freq)
    ang = jnp.concatenate([row[:, None] * inv, col[:, None] * inv], axis=-1)
    return jnp.cos(ang), jnp.sin(ang)


def apply_rope(x, rope):
    cos, sin = rope
    cos = cos[:, None, :].astype(x.dtype)
    sin = sin[:, None, :].astype(x.dtype)
    half = x.shape[-1] // 2
    x1, x2 = x[..., :half], x[..., half:]
    return jnp.concatenate([x1 * cos - x2 * sin, x1 * sin + x2 * cos], axis=-1)


def diff_attention(q1, q2, k1, k2, v, lam):
    B, Nq, H, dq = q1.shape
    nb = Nq // Q_BLOCK
    scale = dq ** -0.5
    qb1 = q1.reshape(B, nb, Q_BLOCK, H, dq).transpose(1, 0, 2, 3, 4)
    qb2 = q2.reshape(B, nb, Q_BLOCK, H, dq).transpose(1, 0, 2, 3, 4)

    def block(qs):
        a, b = qs
        s1 = jnp.einsum('bqhd,bkhd->bhqk', a, k1).astype(jnp.float32) * scale
        s2 = jnp.einsum('bqhd,bkhd->bhqk', b, k2).astype(jnp.float32) * scale
        p = jax.nn.softmax(s1, axis=-1) - lam * jax.nn.softmax(s2, axis=-1)
        return jnp.einsum('bhqk,bkhd->bqhd', p.astype(v.dtype), v)

    o = lax.map(block, (qb1, qb2))
    return o.transpose(1, 0, 2, 3, 4).reshape(B, Nq, H, v.shape[-1])


def dense_sink_attention(q, k, v, sink):
    B, N, Hq, hd = q.shape
    Hkv = k.shape[2]
    G = Hq // Hkv
    nb = N // Q_BLOCK
    scale = hd ** -0.5
    qb = q.reshape(B, nb, Q_BLOCK, Hkv, G, hd).transpose(1, 0, 2, 3, 4, 5)
    snk = sink.reshape(Hkv, G, 1, 1).astype(jnp.float32)

    def block(qi):
        s = jnp.einsum('bqkgd,bjkd->bkgqj', qi, k).astype(jnp.float32) * scale
        s = jnp.concatenate([s, jnp.broadcast_to(snk, s.shape[:-1] + (1,))], axis=-1)
        p = jax.nn.softmax(s, axis=-1)[..., :-1]
        return jnp.einsum('bkgqj,bjkd->bqkgd', p.astype(v.dtype), v)

    o = lax.map(block, qb)
    return o.transpose(1, 0, 2, 3, 4, 5).reshape(B, N, Hq, hd)


def band_blocks(t):
    B, N, H, d = t.shape
    nb = N // C_WINDOW
    tp = jnp.pad(t, ((0, 0), (C_WINDOW, C_WINDOW), (0, 0), (0, 0))).reshape(B, nb + 2, C_WINDOW, H, d)
    return jnp.concatenate([tp[:, :-2], tp[:, 1:-1], tp[:, 2:]], axis=2)


def window_sink_attention(q, k, v, ck, cv, sink):
    B, N, Hq, hd = q.shape
    Hkv = k.shape[2]
    G = Hq // Hkv
    W = C_WINDOW
    nb = N // W
    Sc = ck.shape[1]
    scale = hd ** -0.5
    qb = q.reshape(B, nb, W, Hkv, G, hd)
    kb, vb = band_blocks(k), band_blocks(v)
    s_loc = jnp.einsum('bnqkgd,bnjkd->bnkgqj', qb, kb).astype(jnp.float32) * scale
    qpos = jnp.arange(N).reshape(nb, W)[:, :, None]
    kpos = (jnp.arange(nb)[:, None] * W - W + jnp.arange(3 * W)[None, :])[:, None, :]
    valid = (kpos >= 0) & (kpos < N) & (jnp.abs(kpos - qpos) <= W)
    s_loc = jnp.where(valid[None, :, None, None], s_loc, NEG)
    s_ctx = jnp.einsum('bnqkgd,bjkd->bnkgqj', qb, ck).astype(jnp.float32) * scale
    s_snk = jnp.broadcast_to(sink.reshape(Hkv, G, 1, 1).astype(jnp.float32), s_ctx.shape[:-1] + (1,))
    p = jax.nn.softmax(jnp.concatenate([s_loc, s_ctx, s_snk], axis=-1), axis=-1).astype(v.dtype)
    o = (jnp.einsum('bnkgqj,bnjkd->bnqkgd', p[..., :3 * W], vb)
         + jnp.einsum('bnkgqj,bjkd->bnqkgd', p[..., 3 * W:3 * W + Sc], cv))
    return o.reshape(B, N, Hq, hd)


def fourier_mix(u, w):
    B, N, _ = u.shape
    g = u.astype(jnp.float32).reshape(B, N, B_GROUPS, B_GD).transpose(0, 2, 1, 3)
    f = jnp.fft.fft2(g, axes=(-2, -1), norm='ortho').real
    return f.transpose(0, 2, 1, 3).reshape(B, N, GROUP_WIDTH).astype(u.dtype) @ w


def chunk_sgu(u, v, g, b, ws, bs):
    B, N, _ = u.shape
    vn = layer_norm_affine(v, g, b).reshape(B, N // CHUNK, CHUNK, D_HEADS, D_HD)
    mixed = jnp.einsum('hpq,bcqhd->bcphd', ws, vn) + bs.T[:, :, None]
    return u * mixed.reshape(B, N, GROUP_WIDTH)


def swiglu(h, wg, wu, wd):
    return (jax.nn.silu(h @ wg) * (h @ wu)) @ wd


def moe_swiglu(h, w_router, b_router, wg, wu, wd):
    logits = (h @ w_router).astype(jnp.float32) + b_router.astype(jnp.float32)
    top_val, top_idx = lax.top_k(logits, TOP_K)
    top_w = jax.nn.softmax(top_val, axis=-1)
    gates = jnp.einsum('bnk,bnke->bne', top_w,
                       jax.nn.one_hot(top_idx, N_EXPERTS, dtype=jnp.float32)).astype(h.dtype)
    out = jnp.zeros_like(h)
    for e in range(N_EXPERTS):
        out = out + gates[..., e:e + 1] * swiglu(h, wg[e], wu[e], wd[e])
    return out


def token_mixers(h, p, lam_init, ctx_kv, ropes):
    B, N, _ = h.shape
    proj = h @ p['w_in']
    aq1, aq2, ak1, ak2, av, bu, cq, ck, cv, du, dv = jnp.split(proj, IN_SPLITS, axis=-1)
    aq1, aq2, ak1, ak2 = (t.reshape(B, N, A_HEADS, A_QK) for t in (aq1, aq2, ak1, ak2))
    av = av.reshape(B, N, A_HEADS, A_V)
    cq = cq.reshape(B, N, C_HEADS, C_HD)
    ck = ck.reshape(B, N, C_KV_HEADS, C_HD)
    cv = cv.reshape(B, N, C_KV_HEADS, C_HD)
    lq1, lk1, lq2, lk2 = (p['lam'][i].astype(jnp.float32) for i in range(4))
    lam = jnp.exp(jnp.sum(lq1 * lk1)) - jnp.exp(jnp.sum(lq2 * lk2)) + lam_init
    if ctx_kv is None:
        oa = diff_attention(aq1, aq2, ak1, ak2, av, lam)
        oc = dense_sink_attention(cq, ck, cv, p['c_sink'])
        new_kv = (jnp.concatenate([ak1, ak2], axis=-1), av, ck, cv)
    else:
        ctx_k_a, ctx_v_a, ctx_k_c, ctx_v_c = ctx_kv
        rope_a, rope_c = ropes
        aq1, aq2, ak1, ak2 = (apply_rope(t, rope_a) for t in (aq1, aq2, ak1, ak2))
        cq, ck = apply_rope(cq, rope_c), apply_rope(ck, rope_c)
        k1 = jnp.concatenate([ak1, ctx_k_a[..., :A_QK]], axis=1)
        k2 = jnp.concatenate([ak2, ctx_k_a[..., A_QK:]], axis=1)
        va = jnp.concatenate([av, ctx_v_a], axis=1)
        oa = diff_attention(aq1, aq2, k1, k2, va, lam)
        oc = window_sink_attention(cq, ck, cv, ctx_k_c, ctx_v_c, p['c_sink'])
        new_kv = None
    oa = rms_norm(oa, p['a_subln']) * (1.0 - lam_init)
    ob = fourier_mix(bu, p['w_fnet'])
    od = chunk_sgu(du, dv, p['d_norm_g'], p['d_norm_b'], p['d_ws'], p['d_bs'])
    mix = jnp.concatenate([oa.reshape(B, N, -1), ob, oc.reshape(B, N, -1), od], axis=-1)
    return mix @ p['w_out'], new_kv


def trunk_layer(x, mod, p, lam_init, ctx_kv, ropes):
    sh1, sc1, g1, sh2, sc2, g2 = jnp.split(mod[:, None, :], 6, axis=-1)
    h = layer_norm(x) * (1 + sc1) + sh1
    mix, new_kv = token_mixers(h, p, lam_init, ctx_kv, ropes)
    x = layer_norm_affine(ALPHA * x + g1 * mix, p['ln1_g'], p['ln1_b'])
    h = layer_norm(x) * (1 + sc2) + sh2
    if 'w_router' in p:
        f = moe_swiglu(h, p['w_router'], p['b_router'], p['wg'], p['wu'], p['wd'])
    else:
        f = swiglu(h, p['wg'], p['wu'], p['wd'])
    x = layer_norm_affine(ALPHA * x + g2 * f, p['ln2_g'], p['ln2_b'])
    return x, new_kv


def setup_inputs(seed: int = 0) -> dict:
    key = jax.random.key(seed)
    ks = iter(jax.random.split(key, 48))
    nrm = lambda shape, s: jax.random.normal(next(ks), shape, jnp.float32) * s
    D = D_MODEL
    return {
        'x_prompt': nrm((BATCH, SEQ, D), 1.0),
        'x_sample': nrm((DEC_BATCH, DEC_SEQ, D), 1.0),
        'cache_a_k': nrm((DEC_BATCH, DEPTH, PAST_LEN, A_HEADS, 2 * A_QK), 1.0),
        'cache_a_v': nrm((DEC_BATCH, DEPTH, PAST_LEN, A_HEADS, A_V), 1.0),
        'cache_c_k': nrm((DEC_BATCH, DEPTH, PAST_LEN, C_KV_HEADS, C_HD), 1.0),
        'cache_c_v': nrm((DEC_BATCH, DEPTH, PAST_LEN, C_KV_HEADS, C_HD), 1.0),
        'c': nrm((DEC_BATCH, D), 1.0),
        'c_ctx': nrm((D,), 1.0),
        'w_ada': nrm((DEPTH, D, 6 * D), 0.5 * D ** -0.5),
        'b_ada': nrm((DEPTH, 6 * D), 0.01),
        'w_in': nrm((DEPTH, D, IN_WIDTH), D ** -0.5),
        'lam': nrm((DEPTH, 4, A_QK), 0.1),
        'a_subln': 1.0 + nrm((DEPTH, A_V), 0.01),
        'w_fnet': nrm((DEPTH, GROUP_WIDTH, GROUP_WIDTH), GROUP_WIDTH ** -0.5),
        'c_sink': nrm((DEPTH, C_HEADS), 0.5),
        'd_norm_g': 1.0 + nrm((DEPTH, GROUP_WIDTH), 0.01),
        'd_norm_b': nrm((DEPTH, GROUP_WIDTH), 0.01),
        'd_ws': nrm((DEPTH, D_HEADS, CHUNK, CHUNK), CHUNK ** -0.5),
        'd_bs': 1.0 + nrm((DEPTH, D_HEADS, CHUNK), 0.01),
        'w_out': nrm((DEPTH, D, D), BETA * D ** -0.5),
        'ln1_g': 1.0 + nrm((DEPTH, D), 0.01),
        'ln1_b': nrm((DEPTH, D), 0.01),
        'ln2_g': 1.0 + nrm((DEPTH, D), 0.01),
        'ln2_b': nrm((DEPTH, D), 0.01),
        'ffn_wg': nrm((N_DENSE, D, D_FF), D ** -0.5),
        'ffn_wu': nrm((N_DENSE, D, D_FF), D ** -0.5),
        'ffn_wd': nrm((N_DENSE, D_FF, D), BETA * D_FF ** -0.5),
        'w_router': nrm((N_MOE, D, N_EXPERTS), D ** -0.5),
        'b_router': nrm((N_MOE, N_EXPERTS), 0.01),
        'moe_wg': nrm((N_MOE, N_EXPERTS, D, D_FF_EXPERT), D ** -0.5),
        'moe_wu': nrm((N_MOE, N_EXPERTS, D, D_FF_EXPERT), D ** -0.5),
        'moe_wd': nrm((N_MOE, N_EXPERTS, D_FF_EXPERT, D), BETA * D_FF_EXPERT ** -0.5),
    }


def reference(x_prompt, x_sample, cache_a_k, cache_a_v, cache_c_k, cache_c_v, c, c_ctx,
              w_ada, b_ada, w_in, lam, a_subln, w_fnet, c_sink, d_norm_g, d_norm_b, d_ws, d_bs,
              w_out, ln1_g, ln1_b, ln2_g, ln2_b, ffn_wg, ffn_wu, ffn_wd,
              w_router, b_router, moe_wg, moe_wu, moe_wd):
    n_lat = x_sample.shape[1]
    ropes = (axial_rope(n_lat, A_QK), axial_rope(n_lat, C_HD))
    silu_c = jax.nn.silu(c)
    silu_ctx = jax.nn.silu(c_ctx)
    y_p, y_s = x_prompt, x_sample
    ak_list, av_list, ck_list, cv_list = [], [], [], []
    for l in range(DEPTH):
        p = {'w_in': w_in[l], 'lam': lam[l], 'a_subln': a_subln[l], 'w_fnet': w_fnet[l],
             'c_sink': c_sink[l], 'd_norm_g': d_norm_g[l], 'd_norm_b': d_norm_b[l],
             'd_ws': d_ws[l], 'd_bs': d_bs[l], 'w_out': w_out[l],
             'ln1_g': ln1_g[l], 'ln1_b': ln1_b[l], 'ln2_g': ln2_g[l], 'ln2_b': ln2_b[l]}
        j = l // 2
        if l % 2 == 0:
            p.update({'wg': ffn_wg[j], 'wu': ffn_wu[j], 'wd': ffn_wd[j]})
        else:
            p.update({'w_router': w_router[j], 'b_router': b_router[j],
                      'wg': moe_wg[j], 'wu': moe_wu[j], 'wd': moe_wd[j]})
        lam_init = 0.8 - 0.6 * math.exp(-0.3 * l)
        mod_ctx = (silu_ctx @ w_ada[l] + b_ada[l])[None]
        mod_lat = silu_c @ w_ada[l] + b_ada[l]
        y_p, (nak, nav, nck, ncv) = trunk_layer(y_p, mod_ctx, p, lam_init, None, None)
        ak_list.append(nak)
        av_list.append(nav)
        ck_list.append(nck)
        cv_list.append(ncv)
        ctx_kv = (cache_a_k[:, l], cache_a_v[:, l], cache_c_k[:, l], cache_c_v[:, l])
        y_s, _ = trunk_layer(y_s, mod_lat, p, lam_init, ctx_kv, ropes)
    new_cache_a_k = jnp.stack(ak_list, axis=1)
    new_cache_a_v = jnp.stack(av_list, axis=1)
    new_cache_c_k = jnp.stack(ck_list, axis=1)
    new_cache_c_v = jnp.stack(cv_list, axis=1)
    return (y_p, y_s, new_cache_a_k, new_cache_a_v, new_cache_c_k, new_cache_c_v)
```

```python
import functools
import math

import jax
import jax.numpy as jnp
from jax import lax
from jax.experimental import pallas as pl
from jax.experimental.pallas import tpu as pltpu

F32 = jnp.float32
BF16 = jnp.bfloat16

GRID_W = 64
A_V = 128
A_QK = 64
A_HEADS = 4
B_GD = 128
B_GROUPS = 4
C_HD = 128
C_HEADS = 4
C_KV_HEADS = 2
C_GROUP = C_HEADS // C_KV_HEADS
C_WINDOW = 128
D_HD = 128
D_HEADS = 4
CHUNK = 128
GROUP_WIDTH = 512
N_EXPERTS = 8
ROPE_BASE = 10000.0
LN_EPS = 1e-6
SUBLN_EPS = 1e-5
NEG = -1e30
TOTAL_DEPTH = 4
ALPHA = (2.0 * TOTAL_DEPTH) ** 0.25

COL_AQ = 0
COL_AK = 512
COL_AV = 1024
COL_BU = 1536
COL_CQ = 2048
COL_CK = 2560
COL_CV = 2816
COL_DU = 3072
COL_DV = 3584

LANES = 128
V7X_VMEM_BYTES = 64 * 1024 * 1024
VMEM_LIMIT = V7X_VMEM_BYTES - 8 * 1024 * 1024


def _cparams(*sem):
    return pltpu.CompilerParams(dimension_semantics=sem, vmem_limit_bytes=VMEM_LIMIT)


def _dot(a, b):
    return jnp.dot(a, b, preferred_element_type=F32)


def _dot_nt(a, b):
    return lax.dot_general(a, b, (((1,), (1,)), ((), ())), preferred_element_type=F32)


def _split(a):
    hi = a.astype(BF16)
    lo = (a - hi.astype(F32)).astype(BF16)
    return hi, lo


def _dot3(a_hi, a_lo, b_hi, b_lo):
    return _dot(a_hi, b_hi) + (_dot(a_hi, b_lo) + _dot(a_lo, b_hi))


def _ln_rows(x):
    xc = x - jnp.mean(x, axis=-1, keepdims=True)
    return xc * lax.rsqrt(jnp.mean(xc * xc, axis=-1, keepdims=True) + LN_EPS)


def _silu(x):
    return x / (1.0 + jnp.exp(-x))


def _mod_row(i, tm, tp, ns):
    return jnp.where(i * tm < tp, 0, 1 + (i * tm - tp) // ns)


def _modulation(c_all, w_ada, b_ada):
    depth, d, n6 = w_ada.shape
    rows = c_all.shape[0]
    tn = 512

    def body(c_ref, w_ref, b_ref, o_ref):
        s = _silu(c_ref[...]).astype(BF16)
        o_ref[0] = _dot(s, w_ref[0].astype(BF16)) + b_ref[0]

    return pl.pallas_call(
        body,
        out_shape=jax.ShapeDtypeStruct((depth, rows, n6), F32),
        grid=(depth, n6 // tn),
        in_specs=[pl.BlockSpec((rows, d), lambda l, j: (0, 0)),
                  pl.BlockSpec((1, d, tn), lambda l, j: (l, 0, j)),
                  pl.BlockSpec((1, 1, tn), lambda l, j: (l, 0, j))],
        out_specs=pl.BlockSpec((1, rows, tn), lambda l, j: (l, 0, j)),
        compiler_params=_cparams("parallel", "parallel"),
        name="modulation",
    )(c_all, w_ada, b_ada.reshape(depth, 1, n6))


def _in_proj(x, mods_l, w, tp, ns, tm, tn):
    t, d = x.shape
    n = w.shape[1]

    def body(x_ref, m_ref, w_ref, o_ref, h_ref):
        @pl.when(pl.program_id(1) == 0)
        def _():
            h = _ln_rows(x_ref[...]) * (1.0 + m_ref[0, 1:2, :]) + m_ref[0, 0:1, :]
            h_ref[...] = h.astype(BF16)

        o_ref[...] = _dot(h_ref[...], w_ref[...])

    row = functools.partial(_mod_row, tm=tm, tp=tp, ns=ns)
    return pl.pallas_call(
        body,
        out_shape=jax.ShapeDtypeStruct((t, n), F32),
        grid=(t // tm, n // tn),
        in_specs=[pl.BlockSpec((tm, d), lambda i, j: (i, 0)),
                  pl.BlockSpec((1, 6, d), lambda i, j: (row(i), 0, 0)),
                  pl.BlockSpec((d, tn), lambda i, j: (0, j))],
        out_specs=pl.BlockSpec((tm, tn), lambda i, j: (i, j)),
        scratch_shapes=[pltpu.VMEM((tm, d), BF16)],
        compiler_params=_cparams("parallel", "arbitrary"),
        name="in_proj",
    )(x, mods_l, w)


def _lam_value(lam_ref, lam_init):
    lv = lam_ref[...]
    d1 = jnp.sum(lv[0:1] * lv[1:2], axis=-1, keepdims=True)
    d2 = jnp.sum(lv[2:3] * lv[3:4], axis=-1, keepdims=True)
    return jnp.exp(d1) - jnp.exp(d2) + lam_init


def _softmax(s):
    e = jnp.exp(s - jnp.max(s, axis=-1, keepdims=True))
    return e / jnp.sum(e, axis=-1, keepdims=True)


def _diff_attend(q, kb, vb, lam):
    lane = lax.broadcasted_iota(jnp.int32, q.shape, 1)
    q1 = jnp.where(lane < A_QK, q, 0.0).astype(BF16)
    q2 = jnp.where(lane >= A_QK, q, 0.0).astype(BF16)
    scale = A_QK ** -0.5
    s1 = _dot_nt(q1, kb) * scale
    s2 = _dot_nt(q2, kb) * scale
    p = _softmax(s1) - lam * _softmax(s2)
    return _dot(p.astype(BF16), vb)


def _sub_ln(o, g, lam_init):
    r = lax.rsqrt(jnp.mean(o * o, axis=-1, keepdims=True) + SUBLN_EPS)
    return (o * r) * g * (1.0 - lam_init)


def _rope(x, cos, sin_signed, half):
    if 2 * half == LANES:
        swapped = pltpu.roll(x, half, axis=1)
    else:
        lane = lax.broadcasted_iota(jnp.int32, x.shape, 1)
        first = (lane % (2 * half)) < half
        swapped = jnp.where(first, pltpu.roll(x, LANES - half, axis=1),
                            pltpu.roll(x, half, axis=1))
    return x * cos + swapped * sin_signed


def _rope_tables(n_tokens, head_dim):
    rows = n_tokens // GRID_W
    row = jnp.repeat(jnp.arange(rows, dtype=F32), GRID_W)
    col = jnp.tile(jnp.arange(GRID_W, dtype=F32), rows)
    n_freq = head_dim // 4
    inv = jnp.power(ROPE_BASE, -jnp.arange(n_freq, dtype=F32) / n_freq)
    ang = jnp.concatenate([row[:, None] * inv, col[:, None] * inv], axis=-1)
    cos, sin = jnp.cos(ang), jnp.sin(ang)
    reps = LANES // head_dim
    cos_t = jnp.tile(jnp.concatenate([cos, cos], axis=-1), (1, reps))
    sin_t = jnp.tile(jnp.concatenate([-sin, sin], axis=-1), (1, reps))
    return cos_t, sin_t


def _dft_tables(n):
    j = lax.broadcasted_iota(jnp.int32, (n, n), 0)
    k = lax.broadcasted_iota(jnp.int32, (n, n), 1)
    ang = ((j * k) % n).astype(F32) * (2.0 * math.pi / n)
    w = jnp.concatenate([jnp.cos(ang), -jnp.sin(ang)], axis=1) * (n ** -0.5)
    return _split(w)


def _dft_channel_tables():
    n = B_GD
    j = lax.broadcasted_iota(jnp.int32, (n, n), 0)
    k = lax.broadcasted_iota(jnp.int32, (n, n), 1)
    ang = ((j * k) % n).astype(F32) * (2.0 * math.pi / n)
    w = jnp.concatenate([jnp.cos(ang), jnp.sin(ang)], axis=1) * (n ** -0.5)
    return _split(w)


def _fourier_stage1(u, csh, csl):
    uh, ul = _split(u)
    t = _dot3(uh, ul, csh, csl)
    return jnp.concatenate([t[:, :B_GD], t[:, B_GD:]], axis=0)


def _prompt_mixers(proj, lam_l, subln_l, sink_l, w2h, w2l, csh, csl, wf, lam_init, nbatch, n):
    t = nbatch * n

    def body(p_ref, lam_ref, g_ref, sink_ref, w2h_ref, w2l_ref, csh_ref, csl_ref, wf_ref,
             oa_ref, ob_ref, oc_ref):
        lam = _lam_value(lam_ref, lam_init)
        for h in range(A_HEADS):
            sl = slice(h * A_V, (h + 1) * A_V)
            q = p_ref[:, COL_AQ + h * A_V:COL_AQ + (h + 1) * A_V]
            kb = p_ref[:, COL_AK + h * A_V:COL_AK + (h + 1) * A_V].astype(BF16)
            vb = p_ref[:, COL_AV + h * A_V:COL_AV + (h + 1) * A_V].astype(BF16)
            o = _diff_attend(q, kb, vb, lam)
            oa_ref[:, sl] = _sub_ln(o, g_ref[...], lam_init).astype(BF16)

        acc = None
        for g in range(B_GROUPS):
            u = p_ref[:, COL_BU + g * B_GD:COL_BU + (g + 1) * B_GD]
            th, tl = _split(_fourier_stage1(u, csh_ref[...], csl_ref[...]))
            f = _dot3(w2h_ref[...], w2l_ref[...], th, tl)
            part = _dot(f.astype(BF16), wf_ref[g * B_GD:(g + 1) * B_GD, :])
            acc = part if acc is None else acc + part
        ob_ref[...] = acc.astype(BF16)

        scale = C_HD ** -0.5
        for hq in range(C_HEADS):
            kvh = hq // C_GROUP
            q = p_ref[:, COL_CQ + hq * C_HD:COL_CQ + (hq + 1) * C_HD].astype(BF16)
            kb = p_ref[:, COL_CK + kvh * C_HD:COL_CK + (kvh + 1) * C_HD].astype(BF16)
            vb = p_ref[:, COL_CV + kvh * C_HD:COL_CV + (kvh + 1) * C_HD].astype(BF16)
            s = _dot_nt(q, kb) * scale
            snk = sink_ref[hq]
            m = jnp.maximum(jnp.max(s, axis=-1, keepdims=True), snk)
            e = jnp.exp(s - m)
            den = jnp.sum(e, axis=-1, keepdims=True) + jnp.exp(snk - m)
            o = _dot((e / den).astype(BF16), vb)
            oc_ref[:, hq * C_HD:(hq + 1) * C_HD] = o.astype(BF16)

    full = lambda shape: pl.BlockSpec(shape, lambda b: (0,) * len(shape))
    out = jax.ShapeDtypeStruct((t, GROUP_WIDTH), BF16)
    ospec = pl.BlockSpec((n, GROUP_WIDTH), lambda b: (b, 0))
    return pl.pallas_call(
        body,
        out_shape=(out, out, out),
        grid=(nbatch,),
        in_specs=[pl.BlockSpec((n, proj.shape[1]), lambda b: (b, 0)),
                  full(lam_l.shape), full(subln_l.shape),
                  pl.BlockSpec(memory_space=pltpu.SMEM),
                  full(w2h.shape), full(w2l.shape), full(csh.shape), full(csl.shape),
                  full(wf.shape)],
        out_specs=(ospec, ospec, ospec),
        compiler_params=_cparams("parallel"),
        name="prompt_mixers",
    )(proj, lam_l, subln_l, sink_l, w2h, w2l, csh, csl, wf)


def _sample_diff_attn(proj, cache_k, cache_v, layer, cos_t, sin_t, lam_l, subln_l,
                      lam_init, tp, nbatch, ns, tq):
    past = cache_k.shape[2]
    nk = ns + past
    qb0 = tp // tq
    qpb = ns // tq
    kb0 = tp // ns

    def body(q_ref, k_ref, v_ref, ck_ref, cv_ref, cq_ref, sq_ref, cosk_ref, sink_ref,
             lam_ref, g_ref, o_ref, kr_ref, vb_ref):
        @pl.when(pl.program_id(2) == 0)
        def _():
            kr_ref[0:ns, :] = _rope(k_ref[...], cosk_ref[...], sink_ref[...], A_QK // 2).astype(BF16)
            kr_ref[ns:nk, :] = ck_ref[...].astype(BF16)
            vb_ref[0:ns, :] = v_ref[...].astype(BF16)
            vb_ref[ns:nk, :] = cv_ref[...].astype(BF16)

        lam = _lam_value(lam_ref, lam_init)
        q = _rope(q_ref[...], cq_ref[...], sq_ref[...], A_QK // 2)
        o = _diff_attend(q, kr_ref[...], vb_ref[...], lam)
        o_ref[...] = _sub_ln(o, g_ref[...], lam_init).astype(BF16)

    cache_spec = pl.BlockSpec((None, None, past, A_V), lambda b, h, i: (b, layer, 0, h))
    full = lambda shape: pl.BlockSpec(shape, lambda b, h, i: (0,) * len(shape))
    return pl.pallas_call(
        body,
        out_shape=jax.ShapeDtypeStruct((nbatch * ns, GROUP_WIDTH), BF16),
        grid=(nbatch, A_HEADS, qpb),
        in_specs=[pl.BlockSpec((tq, A_V), lambda b, h, i: (qb0 + b * qpb + i, COL_AQ // A_V + h)),
                  pl.BlockSpec((ns, A_V), lambda b, h, i: (kb0 + b, COL_AK // A_V + h)),
                  pl.BlockSpec((ns, A_V), lambda b, h, i: (kb0 + b, COL_AV // A_V + h)),
                  cache_spec, cache_spec,
                  pl.BlockSpec((tq, LANES), lambda b, h, i: (i, 0)),
                  pl.BlockSpec((tq, LANES), lambda b, h, i: (i, 0)),
                  full((ns, LANES)), full((ns, LANES)),
                  full(lam_l.shape), full(subln_l.shape)],
        out_specs=pl.BlockSpec((tq, A_V), lambda b, h, i: (b * qpb + i, h)),
        scratch_shapes=[pltpu.VMEM((nk, A_V), BF16), pltpu.VMEM((nk, A_V), BF16)],
        compiler_params=_cparams("parallel", "parallel", "arbitrary"),
        name="sample_diff_attn",
    )(proj, proj, proj, cache_k, cache_v, cos_t, sin_t, cos_t, sin_t, lam_l, subln_l)


def _sample_window_attn(proj, cache_k, cache_v, layer, cos_t, sin_t, sink_l,
                        tp, nbatch, ns):
    w = C_WINDOW
    past = cache_k.shape[2]
    nblk = ns // w
    qb0 = tp // w
    kb0 = tp // ns
    gw = C_GROUP * C_HD
    scale = C_HD ** -0.5

    def body(q_ref, k_ref, v_ref, ck_ref, cv_ref, cq_ref, sq_ref, cosk_ref, sink_ref,
             snk_ref, o_ref, kr_ref, vb_ref):
        kvh = pl.program_id(1)
        i = pl.program_id(2)

        @pl.when(i == 0)
        def _():
            kr_ref[...] = _rope(k_ref[...], cosk_ref[...], sink_ref[...], C_HD // 2).astype(BF16)
            vb_ref[...] = v_ref[...].astype(BF16)

        qs = [_rope(q_ref[:, g * C_HD:(g + 1) * C_HD], cq_ref[...], sq_ref[...], C_HD // 2)
              for g in range(C_GROUP)]
        qq = jnp.concatenate(qs, axis=0).astype(BF16)
        start = pl.multiple_of(jnp.clip((i - 1) * w, 0, ns - 3 * w), w)
        kw = kr_ref[pl.ds(start, 3 * w), :]
        vw = vb_ref[pl.ds(start, 3 * w), :]
        s_loc = _dot_nt(qq, kw) * scale
        rows = lax.broadcasted_iota(jnp.int32, s_loc.shape, 0)
        cols = lax.broadcasted_iota(jnp.int32, s_loc.shape, 1)
        qpos = i * w + (rows & (w - 1))
        kpos = start + cols
        s_loc = jnp.where(jnp.abs(kpos - qpos) <= w, s_loc, NEG)
        s_ctx = _dot_nt(qq, ck_ref[...].astype(BF16)) * scale
        rcol = lax.broadcasted_iota(jnp.int32, (C_GROUP * w, 1), 0)
        snk = jnp.zeros((C_GROUP * w, 1), F32)
        for g in range(C_GROUP):
            snk = jnp.where(rcol >= g * w, snk_ref[kvh * C_GROUP + g], snk)
        m = jnp.maximum(jnp.maximum(jnp.max(s_loc, axis=-1, keepdims=True),
                                    jnp.max(s_ctx, axis=-1, keepdims=True)), snk)
        e_loc = jnp.exp(s_loc - m)
        e_ctx = jnp.exp(s_ctx - m)
        den = (jnp.sum(e_loc, axis=-1, keepdims=True) + jnp.sum(e_ctx, axis=-1, keepdims=True)
               + jnp.exp(snk - m))
        o = (_dot((e_loc / den).astype(BF16), vw)
             + _dot((e_ctx / den).astype(BF16), cv_ref[...].astype(BF16)))
        for g in range(C_GROUP):
            o_ref[:, g * C_HD:(g + 1) * C_HD] = o[g * w:(g + 1) * w, :].astype(BF16)

    cache_spec = pl.BlockSpec((None, None, past, C_HD), lambda b, h, i: (b, layer, 0, h))
    full = lambda shape: pl.BlockSpec(shape, lambda b, h, i: (0,) * len(shape))
    return pl.pallas_call(
        body,
        out_shape=jax.ShapeDtypeStruct((nbatch * ns, GROUP_WIDTH), BF16),
        grid=(nbatch, C_KV_HEADS, nblk),
        in_specs=[pl.BlockSpec((w, gw), lambda b, h, i: (qb0 + b * nblk + i, COL_CQ // gw + h)),
                  pl.BlockSpec((ns, C_HD), lambda b, h, i: (kb0 + b, COL_CK // C_HD + h)),
                  pl.BlockSpec((ns, C_HD), lambda b, h, i: (kb0 + b, COL_CV // C_HD + h)),
                  cache_spec, cache_spec,
                  pl.BlockSpec((w, LANES), lambda b, h, i: (i, 0)),
                  pl.BlockSpec((w, LANES), lambda b, h, i: (i, 0)),
                  full((ns, LANES)), full((ns, LANES)),
                  pl.BlockSpec(memory_space=pltpu.SMEM)],
        out_specs=pl.BlockSpec((w, gw), lambda b, h, i: (b * nblk + i, h)),
        scratch_shapes=[pltpu.VMEM((ns, C_HD), BF16), pltpu.VMEM((ns, C_HD), BF16)],
        compiler_params=_cparams("parallel", "parallel", "arbitrary"),
        name="sample_window_attn",
    )(proj, proj, proj, cache_k, cache_v, cos_t, sin_t, cos_t, sin_t, sink_l)


def _sample_fourier(proj, w2h, w2l, csh, csl, wf, tp, nbatch, ns, tr):
    rpb = ns // tr
    kb0 = tp // ns

    def body(u_ref, w2h_ref, w2l_ref, csh_ref, csl_ref, wf_ref, o_ref, th_ref, tl_ref):
        @pl.when(pl.program_id(1) == 0)
        def _():
            for g in range(B_GROUPS):
                sl = slice(g * B_GD, (g + 1) * B_GD)
                th, tl = _split(_fourier_stage1(u_ref[:, sl], csh_ref[...], csl_ref[...]))
                th_ref[:, sl] = th
                tl_ref[:, sl] = tl

        f = _dot3(w2h_ref[...], w2l_ref[...], th_ref[...], tl_ref[...])
        o_ref[...] = _dot(f.astype(BF16), wf_ref[...]).astype(BF16)

    full = lambda shape: pl.BlockSpec(shape, lambda b, r: (0,) * len(shape))
    return pl.pallas_call(
        body,
        out_shape=jax.ShapeDtypeStruct((nbatch * ns, GROUP_WIDTH), BF16),
        grid=(nbatch, rpb),
        in_specs=[pl.BlockSpec((ns, GROUP_WIDTH), lambda b, r: (kb0 + b, COL_BU // GROUP_WIDTH)),
                  pl.BlockSpec((tr, 2 * ns), lambda b, r: (r, 0)),
                  pl.BlockSpec((tr, 2 * ns), lambda b, r: (r, 0)),
                  full(csh.shape), full(csl.shape), full(wf.shape)],
        out_specs=pl.BlockSpec((tr, GROUP_WIDTH), lambda b, r: (b * rpb + r, 0)),
        scratch_shapes=[pltpu.VMEM((2 * ns, GROUP_WIDTH), BF16),
                        pltpu.VMEM((2 * ns, GROUP_WIDTH), BF16)],
        compiler_params=_cparams("parallel", "arbitrary"),
        name="sample_fourier",
    )(proj, w2h, w2l, csh, csl, wf)


def _spatial_gate(proj, g, b, ws, bst, tr):
    t = proj.shape[0]

    def body(u_ref, v_ref, g_ref, b_ref, ws_ref, bs_ref, o_ref):
        vn = (_ln_rows(v_ref[...]) * g_ref[...] + b_ref[...]).astype(BF16)
        for c in range(tr // CHUNK):
            rs = slice(c * CHUNK, (c + 1) * CHUNK)
            for h in range(D_HEADS):
                cs = slice(h * D_HD, (h + 1) * D_HD)
                mixed = _dot(ws_ref[h], vn[rs, cs]) + bs_ref[:, h:h + 1]
                o_ref[rs, cs] = (u_ref[rs, cs] * mixed).astype(BF16)

    full = lambda shape: pl.BlockSpec(shape, lambda i: (0,) * len(shape))
    return pl.pallas_call(
        body,
        out_shape=jax.ShapeDtypeStruct((t, GROUP_WIDTH), BF16),
        grid=(t // tr,),
        in_specs=[pl.BlockSpec((tr, GROUP_WIDTH), lambda i: (i, COL_DU // GROUP_WIDTH)),
                  pl.BlockSpec((tr, GROUP_WIDTH), lambda i: (i, COL_DV // GROUP_WIDTH)),
                  full(g.shape), full(b.shape), full(ws.shape), full(bst.shape)],
        out_specs=pl.BlockSpec((tr, GROUP_WIDTH), lambda i: (i, 0)),
        compiler_params=_cparams("parallel"),
        name="spatial_gate",
    )(proj, proj, g, b, ws, bst)


def _post_ln(x, f, gate, g, b):
    return _ln_rows(ALPHA * x + gate * f) * g + b


def _out_proj(abc_p, abc_s, od, w, x, mods_l, g1, b1, router, tp, ns, tm):
    t, d = x.shape
    with_router = router is not None
    np_tiles = tp // tm

    def body(ap_ref, bp_ref, cp_ref, as_ref, bs_ref, cs_ref, od_ref, w_ref, x_ref, m_ref,
             g_ref, b_ref, *rest):
        if with_router:
            wrh_ref, wrl_ref, br_ref, x1_ref, h2_ref, lg_ref = rest
        else:
            x1_ref, h2_ref = rest
        gw = GROUP_WIDTH
        is_ctx = pl.program_id(0) < np_tiles
        mix = _dot(jnp.where(is_ctx, ap_ref[...], as_ref[...]), w_ref[0:gw, :])
        mix += _dot(jnp.where(is_ctx, bp_ref[...], bs_ref[...]), w_ref[gw:2 * gw, :])
        mix += _dot(jnp.where(is_ctx, cp_ref[...], cs_ref[...]), w_ref[2 * gw:3 * gw, :])
        mix += _dot(od_ref[...], w_ref[3 * gw:4 * gw, :])
        x1 = _post_ln(x_ref[...], mix, m_ref[0, 2:3, :], g_ref[...], b_ref[...])
        x1_ref[...] = x1
        h2 = _ln_rows(x1) * (1.0 + m_ref[0, 4:5, :]) + m_ref[0, 3:4, :]
        h2_ref[...] = h2.astype(h2_ref.dtype)
        if with_router:
            hh, hl = _split(h2)
            lg_ref[...] = _dot3(hh, hl, wrh_ref[...], wrl_ref[...]) + br_ref[...]

    row = functools.partial(_mod_row, tm=tm, tp=tp, ns=ns)
    full = lambda shape: pl.BlockSpec(shape, lambda i: (0,) * len(shape))
    tile = lambda n: pl.BlockSpec((tm, n), lambda i: (i, 0))
    ctx_tile = pl.BlockSpec((tm, GROUP_WIDTH), lambda i: (jnp.minimum(i, np_tiles - 1), 0))
    lat_tile = pl.BlockSpec((tm, GROUP_WIDTH), lambda i: (jnp.maximum(i - np_tiles, 0), 0))
    in_specs = [ctx_tile] * 3 + [lat_tile] * 3 + [
        tile(GROUP_WIDTH), full(w.shape), tile(d),
        pl.BlockSpec((1, 6, d), lambda i: (row(i), 0, 0)), full(g1.shape), full(b1.shape)]
    args = [*abc_p, *abc_s, od, w, x, mods_l, g1, b1]
    out_shape = [jax.ShapeDtypeStruct((t, d), F32),
                 jax.ShapeDtypeStruct((t, d), F32 if with_router else BF16)]
    out_specs = [tile(d), tile(d)]
    if with_router:
        in_specs += [full(r.shape) for r in router]
        args += list(router)
        out_shape.append(jax.ShapeDtypeStruct((t, LANES), F32))
        out_specs.append(tile(LANES))
    return pl.pallas_call(
        body, out_shape=tuple(out_shape), grid=(t // tm,),
        in_specs=in_specs, out_specs=tuple(out_specs),
        compiler_params=_cparams("parallel"),
        name="out_proj",
    )(*args)


def _ffn_dense(h2, wg, wu, wd, x1, mods_l, g2, b2, tp, ns, tm, tf):
    t, d = x1.shape
    nf = wg.shape[1] // tf

    def body(h_ref, wg_ref, wu_ref, wd_ref, x_ref, m_ref, g_ref, b_ref, o_ref, acc_ref):
        f = pl.program_id(1)
        h = h_ref[...]
        a = (_silu(_dot(h, wg_ref[...])) * _dot(h, wu_ref[...])).astype(BF16)
        part = _dot(a, wd_ref[...])

        @pl.when(f == 0)
        def _():
            acc_ref[...] = part

        @pl.when(f > 0)
        def _():
            acc_ref[...] += part

        @pl.when(f == nf - 1)
        def _():
            o_ref[...] = _post_ln(x_ref[...], acc_ref[...], m_ref[0, 5:6, :], g_ref[...], b_ref[...])

    row = functools.partial(_mod_row, tm=tm, tp=tp, ns=ns)
    full = lambda shape: pl.BlockSpec(shape, lambda i, f: (0,) * len(shape))
    return pl.pallas_call(
        body,
        out_shape=jax.ShapeDtypeStruct((t, d), F32),
        grid=(t // tm, nf),
        in_specs=[pl.BlockSpec((tm, d), lambda i, f: (i, 0)),
                  pl.BlockSpec((d, tf), lambda i, f: (0, f)),
                  pl.BlockSpec((d, tf), lambda i, f: (0, f)),
                  pl.BlockSpec((tf, d), lambda i, f: (f, 0)),
                  pl.BlockSpec((tm, d), lambda i, f: (i, 0)),
                  pl.BlockSpec((1, 6, d), lambda i, f: (row(i), 0, 0)),
                  full(g2.shape), full(b2.shape)],
        out_specs=pl.BlockSpec((tm, d), lambda i, f: (i, 0)),
        scratch_shapes=[pltpu.VMEM((tm, d), F32)],
        compiler_params=_cparams("parallel", "arbitrary"),
        name="ffn_dense",
    )(h2, wg, wu, wd, x1, mods_l, g2, b2)


ROUTE_I1, ROUTE_I2, ROUTE_W1, ROUTE_W2, ROUTE_R1, ROUTE_R2 = range(6)


def _route(logits, tr):
    t = logits.shape[0]

    def body(lg_ref, route_ref, cnt_ref, carry_ref):
        @pl.when(pl.program_id(0) == 0)
        def _():
            carry_ref[...] = jnp.zeros_like(carry_ref)

        lane = lax.broadcasted_iota(jnp.int32, (tr, LANES), 1)
        lane_f = lane.astype(F32)
        lg = jnp.where(lane < N_EXPERTS, lg_ref[...], -jnp.inf)
        m1 = jnp.max(lg, axis=-1, keepdims=True)
        i1 = jnp.min(jnp.where(lg == m1, lane_f, float(LANES)), axis=-1, keepdims=True)
        lg2 = jnp.where(lane_f == i1, -jnp.inf, lg)
        m2 = jnp.max(lg2, axis=-1, keepdims=True)
        i2 = jnp.min(jnp.where(lg2 == m2, lane_f, float(LANES)), axis=-1, keepdims=True)
        e2 = jnp.exp(m2 - m1)
        den = 1.0 + e2
        oh1 = lane_f == i1
        oh2 = lane_f == i2
        sel = jnp.where(oh1 | oh2, 1.0, 0.0)
        ri = lax.broadcasted_iota(jnp.int32, (tr, tr), 0)
        ci = lax.broadcasted_iota(jnp.int32, (tr, tr), 1)
        tri = jnp.where(ri > ci, 1.0, 0.0).astype(BF16)
        prefix = _dot(tri, sel.astype(BF16)) + carry_ref[0:1, :]
        r1 = jnp.sum(jnp.where(oh1, prefix, 0.0), axis=-1, keepdims=True)
        r2 = jnp.sum(jnp.where(oh2, prefix, 0.0), axis=-1, keepdims=True)
        carry_ref[0:1, :] = carry_ref[0:1, :] + jnp.sum(sel, axis=0, keepdims=True)
        out = jnp.zeros((tr, LANES), F32)
        for idx, val in ((ROUTE_I1, i1), (ROUTE_I2, i2), (ROUTE_W1, 1.0 / den),
                         (ROUTE_W2, e2 / den), (ROUTE_R1, r1), (ROUTE_R2, r2)):
            out = jnp.where(lane == idx, val, out)
        route_ref[...] = out
        cnt_ref[...] = carry_ref[...]

    return pl.pallas_call(
        body,
        out_shape=(jax.ShapeDtypeStruct((t, LANES), F32), jax.ShapeDtypeStruct((8, LANES), F32)),
        grid=(t // tr,),
        in_specs=[pl.BlockSpec((tr, LANES), lambda i: (i, 0))],
        out_specs=(pl.BlockSpec((tr, LANES), lambda i: (i, 0)),
                   pl.BlockSpec((8, LANES), lambda i: (0, 0))),
        scratch_shapes=[pltpu.VMEM((8, LANES), F32)],
        compiler_params=_cparams("arbitrary"),
        name="route",
    )(logits)


def _gather_rows(src, h, n_rows, rows_per_step):
    d = h.shape[1]
    g = rows_per_step

    def row_copy(src_ref, h_ref, o_ref, sem, r):
        return pltpu.make_async_copy(h_ref.at[pl.ds(src_ref[r], 1)], o_ref.at[pl.ds(r, 1)], sem)

    def body(src_ref, h_ref, o_ref, sem):
        base = pl.program_id(0) * g

        def issue(r, carry):
            row_copy(src_ref, h_ref, o_ref, sem, base + r).start()
            return carry

        def drain(r, carry):
            row_copy(src_ref, h_ref, o_ref, sem, base + r).wait()
            return carry

        lax.fori_loop(0, g, issue, 0)
        lax.fori_loop(0, g, drain, 0)

    return pl.pallas_call(
        body,
        out_shape=jax.ShapeDtypeStruct((n_rows, d), h.dtype),
        grid_spec=pltpu.PrefetchScalarGridSpec(
            num_scalar_prefetch=1, grid=(n_rows // g,),
            in_specs=[pl.BlockSpec(memory_space=pl.ANY)],
            out_specs=pl.BlockSpec(memory_space=pl.ANY),
            scratch_shapes=[pltpu.SemaphoreType.DMA(())]),
        compiler_params=_cparams("arbitrary"),
        name="gather_rows",
    )(src, h)


def _experts(tile_expert, n_used, xs, wg, wu, wd, tm, tf):
    n_rows, d = xs.shape
    nt = n_rows // tm
    nf = wg.shape[2] // tf

    def body(te_ref, nu_ref, x_ref, wg_ref, wu_ref, wd_ref, y_ref, xb_ref):
        i = pl.program_id(0)
        f = pl.program_id(1)

        @pl.when(i < nu_ref[0])
        def _():
            @pl.when(f == 0)
            def _():
                xb_ref[...] = x_ref[...].astype(BF16)

            xb = xb_ref[...]
            a = (_silu(_dot(xb, wg_ref[...])) * _dot(xb, wu_ref[...])).astype(BF16)
            part = _dot(a, wd_ref[...])

            @pl.when(f == 0)
            def _():
                y_ref[...] = part

            @pl.when(f > 0)
            def _():
                y_ref[...] += part

        @pl.when(jnp.logical_and(i >= nu_ref[0], f == 0))
        def _():
            y_ref[...] = jnp.zeros_like(y_ref)

    def tile_idx(i, nu):
        return jnp.minimum(i, nu[0] - 1)

    def f_idx(i, f, nu):
        return jnp.where(i < nu[0], f, nf - 1)

    return pl.pallas_call(
        body,
        out_shape=jax.ShapeDtypeStruct((n_rows, d), F32),
        grid_spec=pltpu.PrefetchScalarGridSpec(
            num_scalar_prefetch=2, grid=(nt, nf),
            in_specs=[pl.BlockSpec((tm, d), lambda i, f, te, nu: (tile_idx(i, nu), 0)),
                      pl.BlockSpec((None, d, tf), lambda i, f, te, nu: (te[i], 0, f_idx(i, f, nu))),
                      pl.BlockSpec((None, d, tf), lambda i, f, te, nu: (te[i], 0, f_idx(i, f, nu))),
                      pl.BlockSpec((None, tf, d), lambda i, f, te, nu: (te[i], f_idx(i, f, nu), 0))],
            out_specs=pl.BlockSpec((tm, d), lambda i, f, te, nu: (i, 0)),
            scratch_shapes=[pltpu.VMEM((tm, d), BF16)]),
        compiler_params=_cparams("arbitrary", "arbitrary"),
        name="experts",
    )(tile_expert, n_used, xs, wg, wu, wd)


def _combine(pos1, pos2, ys, route, x1, mods_l, g2, b2, tp, ns, tc):
    t, d = x1.shape

    def row_copy(pos_ref, ys_ref, buf_ref, sem, tok, r):
        return pltpu.make_async_copy(ys_ref.at[pl.ds(pos_ref[tok], 1)], buf_ref.at[pl.ds(r, 1)], sem)

    def body(p1_ref, p2_ref, ys_ref, rt_ref, x_ref, m_ref, g_ref, b_ref, o_ref,
             y1_ref, y2_ref, sem):
        base = pl.program_id(0) * tc

        def issue(r, carry):
            row_copy(p1_ref, ys_ref, y1_ref, sem.at[0], base + r, r).start()
            row_copy(p2_ref, ys_ref, y2_ref, sem.at[1], base + r, r).start()
            return carry

        def drain(r, carry):
            row_copy(p1_ref, ys_ref, y1_ref, sem.at[0], base + r, r).wait()
            row_copy(p2_ref, ys_ref, y2_ref, sem.at[1], base + r, r).wait()
            return carry

        lax.fori_loop(0, tc, issue, 0)
        lax.fori_loop(0, tc, drain, 0)
        rt = rt_ref[...]
        f = rt[:, ROUTE_W1:ROUTE_W1 + 1] * y1_ref[...] + rt[:, ROUTE_W2:ROUTE_W2 + 1] * y2_ref[...]
        o_ref[...] = _post_ln(x_ref[...], f, m_ref[0, 5:6, :], g_ref[...], b_ref[...])

    row = functools.partial(_mod_row, tm=tc, tp=tp, ns=ns)
    full = lambda shape: pl.BlockSpec(shape, lambda i, p1, p2: (0,) * len(shape))
    return pl.pallas_call(
        body,
        out_shape=jax.ShapeDtypeStruct((t, d), F32),
        grid_spec=pltpu.PrefetchScalarGridSpec(
            num_scalar_prefetch=2, grid=(t // tc,),
            in_specs=[pl.BlockSpec(memory_space=pl.ANY),
                      pl.BlockSpec((tc, LANES), lambda i, p1, p2: (i, 0)),
                      pl.BlockSpec((tc, d), lambda i, p1, p2: (i, 0)),
                      pl.BlockSpec((1, 6, d), lambda i, p1, p2: (row(i), 0, 0)),
                      full(g2.shape), full(b2.shape)],
            out_specs=pl.BlockSpec((tc, d), lambda i, p1, p2: (i, 0)),
            scratch_shapes=[pltpu.VMEM((tc, d), F32), pltpu.VMEM((tc, d), F32),
                            pltpu.SemaphoreType.DMA((2,))]),
        compiler_params=_cparams("arbitrary"),
        name="combine",
    )(pos1, pos2, ys, route, x1, mods_l, g2, b2)


def _ffn_routed(h2, logits, wg, wu, wd, x1, mods_l, g2, b2, tp, ns, tm, tf):
    t = h2.shape[0]
    route, cnt = _route(logits, min(512, t))
    i1 = route[:, ROUTE_I1].astype(jnp.int32)
    i2 = route[:, ROUTE_I2].astype(jnp.int32)
    counts = cnt[0, :N_EXPERTS].astype(jnp.int32)
    padded = (counts + tm - 1) // tm * tm
    ends = jnp.cumsum(padded)
    starts = ends - padded
    pos1 = starts[i1] + route[:, ROUTE_R1].astype(jnp.int32)
    pos2 = starts[i2] + route[:, ROUTE_R2].astype(jnp.int32)
    nt = 2 * t // tm + N_EXPERTS
    n_used = ends[-1] // tm
    tile_start = jnp.arange(nt, dtype=jnp.int32) * tm
    tile_expert = jnp.sum(tile_start[:, None] >= ends[None, :], axis=1).astype(jnp.int32)
    tile_expert = jnp.minimum(tile_expert, tile_expert[n_used - 1])
    tok = jnp.arange(t, dtype=jnp.int32)
    src = jnp.zeros((nt * tm,), jnp.int32).at[pos1].set(tok).at[pos2].set(tok)

    xs = _gather_rows(src, h2, nt * tm, tm)
    ys = _experts(tile_expert, n_used.reshape(1), xs, wg, wu, wd, tm, tf)
    return _combine(pos1, pos2, ys, route, x1, mods_l, g2, b2, tp, ns, min(256, tm))


def _permute_in_cols(w):
    d = w.shape[1]
    qk = w[..., :4 * A_HEADS * A_QK].reshape(w.shape[0], d, 2, 2, A_HEADS, A_QK)
    qk = qk.transpose(0, 1, 2, 4, 3, 5).reshape(w.shape[0], d, 4 * A_HEADS * A_QK)
    return jnp.concatenate([qk, w[..., 4 * A_HEADS * A_QK:]], axis=-1)


def kernel(x_prompt, x_sample, cache_a_k, cache_a_v, cache_c_k, cache_c_v, c, c_ctx, w_ada, b_ada, w_in, lam, a_subln, w_fnet, c_sink, d_norm_g, d_norm_b, d_ws, d_bs, w_out, ln1_g, ln1_b, ln2_g, ln2_b, ffn_wg, ffn_wu, ffn_wd, w_router, b_router, moe_wg, moe_wu, moe_wd):
    nb_p, n_p, d = x_prompt.shape
    nb_s, n_s, _ = x_sample.shape
    depth = w_in.shape[0]
    past = cache_a_k.shape[2]
    tp = nb_p * n_p
    t = tp + nb_s * n_s
    assert tp % n_s == 0 and n_s % 512 == 0 and n_p % CHUNK == 0 and nb_s + 1 <= 8

    x = jnp.concatenate([x_prompt.reshape(tp, d), x_sample.reshape(nb_s * n_s, d)], axis=0)
    c_all = jnp.zeros((8, d), F32).at[0].set(c_ctx).at[1:1 + nb_s].set(c)
    mods = _modulation(c_all, w_ada, b_ada).reshape(depth, 8, 6, d)

    w_in_b = _permute_in_cols(w_in.astype(BF16))
    w_out_b = w_out.astype(BF16)
    w_fnet_b = w_fnet.astype(BF16)
    d_ws_b = d_ws.astype(BF16)
    rope_a = _rope_tables(n_s, A_QK)
    rope_c = _rope_tables(n_s, C_HD)
    csh, csl = _dft_channel_tables()
    w2p_h, w2p_l = _dft_tables(n_p)
    w2s_h, w2s_l = _dft_tables(n_s)
    ca_k = cache_a_k.reshape(nb_s, depth, past, A_HEADS * A_V)
    ca_v = cache_a_v.reshape(nb_s, depth, past, A_HEADS * A_V)
    cc_k = cache_c_k.reshape(nb_s, depth, past, C_KV_HEADS * C_HD)
    cc_v = cache_c_v.reshape(nb_s, depth, past, C_KV_HEADS * C_HD)

    tm = 512
    new_ak, new_av, new_ck, new_cv = [], [], [], []
    for l in range(depth):
        lam_init = 0.8 - 0.6 * math.exp(-0.3 * l)
        proj = _in_proj(x, mods[l], w_in_b[l], tp, n_s, tm, 512)
        new_ak.append(proj[:tp, COL_AK:COL_AV].reshape(nb_p, n_p, A_HEADS, 2 * A_QK))
        new_av.append(proj[:tp, COL_AV:COL_BU].reshape(nb_p, n_p, A_HEADS, A_V))
        new_ck.append(proj[:tp, COL_CK:COL_CV].reshape(nb_p, n_p, C_KV_HEADS, C_HD))
        new_cv.append(proj[:tp, COL_CV:COL_DU].reshape(nb_p, n_p, C_KV_HEADS, C_HD))

        subln = a_subln[l].reshape(1, A_V)
        abc_p = _prompt_mixers(proj, lam[l], subln, c_sink[l], w2p_h, w2p_l, csh, csl,
                               w_fnet_b[l], lam_init, nb_p, n_p)
        oa_s = _sample_diff_attn(proj, ca_k, ca_v, l, rope_a[0], rope_a[1], lam[l], subln,
                                 lam_init, tp, nb_s, n_s, 256)
        oc_s = _sample_window_attn(proj, cc_k, cc_v, l, rope_c[0], rope_c[1], c_sink[l],
                                   tp, nb_s, n_s)
        ob_s = _sample_fourier(proj, w2s_h, w2s_l, csh, csl, w_fnet_b[l], tp, nb_s, n_s, 256)
        abc_s = (oa_s, ob_s, oc_s)
        od = _spatial_gate(proj, d_norm_g[l].reshape(1, -1), d_norm_b[l].reshape(1, -1),
                           d_ws_b[l], d_bs[l].T, 256)

        g1, b1 = ln1_g[l].reshape(1, d), ln1_b[l].reshape(1, d)
        g2, b2 = ln2_g[l].reshape(1, d), ln2_b[l].reshape(1, d)
        j = l // 2
        if l % 2 == 0:
            x1, h2 = _out_proj(abc_p, abc_s, od, w_out_b[l], x, mods[l], g1, b1, None, tp, n_s, 256)
            x = _ffn_dense(h2, ffn_wg[j].astype(BF16), ffn_wu[j].astype(BF16),
                           ffn_wd[j].astype(BF16), x1, mods[l], g2, b2, tp, n_s, tm, 512)
        else:
            wr = jnp.zeros((d, LANES), F32).at[:, :N_EXPERTS].set(w_router[j])
            br = jnp.zeros((1, LANES), F32).at[0, :N_EXPERTS].set(b_router[j])
            router = (*_split(wr), br)
            x1, h2, logits = _out_proj(abc_p, abc_s, od, w_out_b[l], x, mods[l], g1, b1, router,
                                       tp, n_s, 256)
            x = _ffn_routed(h2, logits, moe_wg[j].astype(BF16), moe_wu[j].astype(BF16),
                            moe_wd[j].astype(BF16), x1, mods[l], g2, b2, tp, n_s, tm, 256)

    y_p = x[:tp].reshape(nb_p, n_p, d)
    y_s = x[tp:].reshape(nb_s, n_s, d)
    return (y_p, y_s, jnp.stack(new_ak, axis=1), jnp.stack(new_av, axis=1),
            jnp.stack(new_ck, axis=1), jnp.stack(new_cv, axis=1))
```

```python
import functools
import math

import jax
import jax.numpy as jnp
from jax import lax
from jax.experimental import pallas as pl
from jax.experimental.pallas import tpu as pltpu

F32 = jnp.float32
BF16 = jnp.bfloat16

GRID_W = 64
A_V = 128
A_QK = 64
A_HEADS = 4
B_GD = 128
B_GROUPS = 4
C_HD = 128
C_HEADS = 4
C_KV_HEADS = 2
C_GROUP = C_HEADS // C_KV_HEADS
C_WINDOW = 128
D_HD = 128
D_HEADS = 4
CHUNK = 128
GROUP_WIDTH = 512
N_EXPERTS = 8
ROPE_BASE = 10000.0
LN_EPS = 1e-6
SUBLN_EPS = 1e-5
NEG = -1e30
TOTAL_DEPTH = 4
ALPHA = (2.0 * TOTAL_DEPTH) ** 0.25

COL_AQ = 0
COL_AK = 512
COL_AV = 1024
COL_BU = 1536
COL_CQ = 2048
COL_CK = 2560
COL_CV = 2816
COL_DU = 3072
COL_DV = 3584

LANES = 128
V7X_VMEM_BYTES = 64 * 1024 * 1024
VMEM_LIMIT = V7X_VMEM_BYTES - 8 * 1024 * 1024


def _cparams(*sem):
    return pltpu.CompilerParams(dimension_semantics=sem, vmem_limit_bytes=VMEM_LIMIT)


def _resident(shape):
    return pl.BlockSpec(shape, lambda *_: (0,) * len(shape), pipeline_mode=pl.Buffered(1))


def _dot(a, b):
    return jnp.dot(a, b, preferred_element_type=F32)


def _dot_nt(a, b):
    return lax.dot_general(a, b, (((1,), (1,)), ((), ())), preferred_element_type=F32)


def _split(a):
    hi = a.astype(BF16)
    lo = (a - hi.astype(F32)).astype(BF16)
    return hi, lo


def _dot3(a_hi, a_lo, b_hi, b_lo):
    return _dot(a_hi, b_hi) + (_dot(a_hi, b_lo) + _dot(a_lo, b_hi))


def _ln_rows(x):
    xc = x - jnp.mean(x, axis=-1, keepdims=True)
    return xc * lax.rsqrt(jnp.mean(xc * xc, axis=-1, keepdims=True) + LN_EPS)


def _silu(x):
    return x / (1.0 + jnp.exp(-x))


def _mod_row(i, tm, tp, ns):
    return jnp.where(i * tm < tp, 0, 1 + (i * tm - tp) // ns)


def _modulation(c_all, w_ada, b_ada):
    depth, d, n6 = w_ada.shape
    rows = c_all.shape[0]
    tn = 512

    def body(c_ref, w_ref, b_ref, o_ref):
        s = _silu(c_ref[...]).astype(BF16)
        o_ref[0] = _dot(s, w_ref[0].astype(BF16)) + b_ref[0]

    return pl.pallas_call(
        body,
        out_shape=jax.ShapeDtypeStruct((depth, rows, n6), F32),
        grid=(depth, n6 // tn),
        in_specs=[pl.BlockSpec((rows, d), lambda l, j: (0, 0)),
                  pl.BlockSpec((1, d, tn), lambda l, j: (l, 0, j)),
                  pl.BlockSpec((1, 1, tn), lambda l, j: (l, 0, j))],
        out_specs=pl.BlockSpec((1, rows, tn), lambda l, j: (l, 0, j)),
        compiler_params=_cparams("parallel", "parallel"),
        name="modulation",
    )(c_all, w_ada, b_ada.reshape(depth, 1, n6))


def _in_proj(x, mods_l, w, tp, ns, tm, tn):
    t, d = x.shape
    n = w.shape[1]

    def body(x_ref, m_ref, w_ref, o_ref, h_ref):
        @pl.when(pl.program_id(1) == 0)
        def _():
            h = _ln_rows(x_ref[...]) * (1.0 + m_ref[0, 1:2, :]) + m_ref[0, 0:1, :]
            h_ref[...] = h.astype(BF16)

        o_ref[...] = _dot(h_ref[...], w_ref[...])

    row = functools.partial(_mod_row, tm=tm, tp=tp, ns=ns)
    return pl.pallas_call(
        body,
        out_shape=jax.ShapeDtypeStruct((t, n), F32),
        grid=(t // tm, n // tn),
        in_specs=[pl.BlockSpec((tm, d), lambda i, j: (i, 0)),
                  pl.BlockSpec((1, 6, d), lambda i, j: (row(i), 0, 0)),
                  pl.BlockSpec((d, tn), lambda i, j: (0, j))],
        out_specs=pl.BlockSpec((tm, tn), lambda i, j: (i, j)),
        scratch_shapes=[pltpu.VMEM((tm, d), BF16)],
        compiler_params=_cparams("parallel", "arbitrary"),
        name="in_proj",
    )(x, mods_l, w)


def _lam_value(lam_ref, lam_init):
    lv = lam_ref[...]
    d1 = jnp.sum(lv[0:1] * lv[1:2], axis=-1, keepdims=True)
    d2 = jnp.sum(lv[2:3] * lv[3:4], axis=-1, keepdims=True)
    return jnp.exp(d1) - jnp.exp(d2) + lam_init


def _softmax(s):
    e = jnp.exp(s - jnp.max(s, axis=-1, keepdims=True))
    return e / jnp.sum(e, axis=-1, keepdims=True)


def _diff_attend(q, kb, vb, lam):
    lane = lax.broadcasted_iota(jnp.int32, q.shape, 1)
    q1 = jnp.where(lane < A_QK, q, 0.0).astype(BF16)
    q2 = jnp.where(lane >= A_QK, q, 0.0).astype(BF16)
    scale = A_QK ** -0.5
    s1 = _dot_nt(q1, kb) * scale
    s2 = _dot_nt(q2, kb) * scale
    p = _softmax(s1) - lam * _softmax(s2)
    return _dot(p.astype(BF16), vb)


KEY_CHUNK = 256


def _diff_attend_chunked(q, kr_ref, vb_ref, lam, kc):
    lane = lax.broadcasted_iota(jnp.int32, q.shape, 1)
    qs = q * (A_QK ** -0.5)
    q1 = jnp.where(lane < A_QK, qs, 0.0).astype(BF16)
    q2 = jnp.where(lane >= A_QK, qs, 0.0).astype(BF16)
    chunks = [slice(c * kc, (c + 1) * kc) for c in range(kr_ref.shape[0] // kc)]

    def fold(x, op):
        out = x[:, :LANES]
        for j in range(1, kc // LANES):
            out = op(out, x[:, j * LANES:(j + 1) * LANES])
        return out

    m1 = m2 = None
    for cs in chunks:
        kb = kr_ref[cs, :]
        c1 = fold(_dot_nt(q1, kb), jnp.maximum)
        c2 = fold(_dot_nt(q2, kb), jnp.maximum)
        m1 = c1 if m1 is None else jnp.maximum(m1, c1)
        m2 = c2 if m2 is None else jnp.maximum(m2, c2)
    m1 = jnp.max(m1, axis=-1, keepdims=True)
    m2 = jnp.max(m2, axis=-1, keepdims=True)
    l1 = l2 = o1 = o2 = 0.0
    for cs in chunks:
        kb = kr_ref[cs, :]
        vb = vb_ref[cs, :]
        e1 = jnp.exp(_dot_nt(q1, kb) - m1)
        e2 = jnp.exp(_dot_nt(q2, kb) - m2)
        l1 = l1 + fold(e1, jnp.add)
        l2 = l2 + fold(e2, jnp.add)
        o1 = o1 + _dot(e1.astype(BF16), vb)
        o2 = o2 + _dot(e2.astype(BF16), vb)
    l1 = jnp.sum(l1, axis=-1, keepdims=True)
    l2 = jnp.sum(l2, axis=-1, keepdims=True)
    return o1 / l1 - lam * (o2 / l2)


def _sub_ln(o, g, lam_init):
    r = lax.rsqrt(jnp.mean(o * o, axis=-1, keepdims=True) + SUBLN_EPS)
    return (o * r) * g * (1.0 - lam_init)


def _rope(x, cos, sin_signed, half):
    if 2 * half == LANES:
        swapped = pltpu.roll(x, half, axis=1)
    else:
        lane = lax.broadcasted_iota(jnp.int32, x.shape, 1)
        first = (lane % (2 * half)) < half
        swapped = jnp.where(first, pltpu.roll(x, LANES - half, axis=1),
                            pltpu.roll(x, half, axis=1))
    return x * cos + swapped * sin_signed


def _rope_tables(n_tokens, head_dim):
    rows = n_tokens // GRID_W
    row = jnp.repeat(jnp.arange(rows, dtype=F32), GRID_W)
    col = jnp.tile(jnp.arange(GRID_W, dtype=F32), rows)
    n_freq = head_dim // 4
    inv = jnp.power(ROPE_BASE, -jnp.arange(n_freq, dtype=F32) / n_freq)
    ang = jnp.concatenate([row[:, None] * inv, col[:, None] * inv], axis=-1)
    cos, sin = jnp.cos(ang), jnp.sin(ang)
    reps = LANES // head_dim
    cos_t = jnp.tile(jnp.concatenate([cos, cos], axis=-1), (1, reps))
    sin_t = jnp.tile(jnp.concatenate([-sin, sin], axis=-1), (1, reps))
    return cos_t, sin_t


def _dft_tables(n):
    j = lax.broadcasted_iota(jnp.int32, (n, n), 0)
    k = lax.broadcasted_iota(jnp.int32, (n, n), 1)
    ang = ((j * k) % n).astype(F32) * (2.0 * math.pi / n)
    w = jnp.concatenate([jnp.cos(ang), -jnp.sin(ang)], axis=1) * (n ** -0.5)
    return _split(w)


def _dft_channel_tables():
    n = B_GD
    j = lax.broadcasted_iota(jnp.int32, (n, n), 0)
    k = lax.broadcasted_iota(jnp.int32, (n, n), 1)
    ang = ((j * k) % n).astype(F32) * (2.0 * math.pi / n)
    w = jnp.concatenate([jnp.cos(ang), jnp.sin(ang)], axis=1) * (n ** -0.5)
    return _split(w)


def _fourier_stage1(u, csh, csl):
    uh, ul = _split(u)
    t = _dot3(uh, ul, csh, csl)
    return jnp.concatenate([t[:, :B_GD], t[:, B_GD:]], axis=0)


def _prompt_mixers(proj, lam_l, subln_l, sink_l, w2h, w2l, csh, csl, wf, lam_init, nbatch, n):
    t = nbatch * n

    def body(p_ref, lam_ref, g_ref, sink_ref, w2h_ref, w2l_ref, csh_ref, csl_ref, wf_ref,
             oa_ref, ob_ref, oc_ref):
        lam = _lam_value(lam_ref, lam_init)
        for h in range(A_HEADS):
            sl = slice(h * A_V, (h + 1) * A_V)
            q = p_ref[:, COL_AQ + h * A_V:COL_AQ + (h + 1) * A_V]
            kb = p_ref[:, COL_AK + h * A_V:COL_AK + (h + 1) * A_V].astype(BF16)
            vb = p_ref[:, COL_AV + h * A_V:COL_AV + (h + 1) * A_V].astype(BF16)
            o = _diff_attend(q, kb, vb, lam)
            oa_ref[:, sl] = _sub_ln(o, g_ref[...], lam_init).astype(BF16)

        acc = None
        for g in range(B_GROUPS):
            u = p_ref[:, COL_BU + g * B_GD:COL_BU + (g + 1) * B_GD]
            th, tl = _split(_fourier_stage1(u, csh_ref[...], csl_ref[...]))
            f = _dot3(w2h_ref[...], w2l_ref[...], th, tl)
            part = _dot(f.astype(BF16), wf_ref[g * B_GD:(g + 1) * B_GD, :])
            acc = part if acc is None else acc + part
        ob_ref[...] = acc.astype(BF16)

        scale = C_HD ** -0.5
        for hq in range(C_HEADS):
            kvh = hq // C_GROUP
            q = p_ref[:, COL_CQ + hq * C_HD:COL_CQ + (hq + 1) * C_HD].astype(BF16)
            kb = p_ref[:, COL_CK + kvh * C_HD:COL_CK + (kvh + 1) * C_HD].astype(BF16)
            vb = p_ref[:, COL_CV + kvh * C_HD:COL_CV + (kvh + 1) * C_HD].astype(BF16)
            s = _dot_nt(q, kb) * scale
            snk = sink_ref[hq]
            m = jnp.maximum(jnp.max(s, axis=-1, keepdims=True), snk)
            e = jnp.exp(s - m)
            den = jnp.sum(e, axis=-1, keepdims=True) + jnp.exp(snk - m)
            o = _dot((e / den).astype(BF16), vb)
            oc_ref[:, hq * C_HD:(hq + 1) * C_HD] = o.astype(BF16)

    full = lambda shape: pl.BlockSpec(shape, lambda b: (0,) * len(shape))
    out = jax.ShapeDtypeStruct((t, GROUP_WIDTH), BF16)
    ospec = pl.BlockSpec((n, GROUP_WIDTH), lambda b: (b, 0))
    return pl.pallas_call(
        body,
        out_shape=(out, out, out),
        grid=(nbatch,),
        in_specs=[pl.BlockSpec((n, proj.shape[1]), lambda b: (b, 0)),
                  full(lam_l.shape), full(subln_l.shape),
                  pl.BlockSpec(memory_space=pltpu.SMEM),
                  full(w2h.shape), full(w2l.shape), full(csh.shape), full(csl.shape),
                  full(wf.shape)],
        out_specs=(ospec, ospec, ospec),
        compiler_params=_cparams("parallel"),
        name="prompt_mixers",
    )(proj, lam_l, subln_l, sink_l, w2h, w2l, csh, csl, wf)


def _sample_diff_attn(proj, cache_k, cache_v, layer, cos_t, sin_t, lam_l, subln_l,
                      lam_init, tp, nbatch, ns, tq):
    past = cache_k.shape[2]
    nk = ns + past
    qb0 = tp // tq
    qpb = ns // tq
    kb0 = tp // ns

    def body(q_ref, k_ref, v_ref, ck_ref, cv_ref, cq_ref, sq_ref, cosk_ref, sink_ref,
             lam_ref, g_ref, o_ref, kr_ref, vb_ref):
        @pl.when(pl.program_id(2) == 0)
        def _():
            kr_ref[0:ns, :] = _rope(k_ref[...], cosk_ref[...], sink_ref[...], A_QK // 2).astype(BF16)
            kr_ref[ns:nk, :] = ck_ref[...].astype(BF16)
            vb_ref[0:ns, :] = v_ref[...].astype(BF16)
            vb_ref[ns:nk, :] = cv_ref[...].astype(BF16)

        lam = _lam_value(lam_ref, lam_init)
        q = _rope(q_ref[...], cq_ref[...], sq_ref[...], A_QK // 2)
        o = _diff_attend_chunked(q, kr_ref, vb_ref, lam, KEY_CHUNK)
        o_ref[...] = _sub_ln(o, g_ref[...], lam_init).astype(BF16)

    assert nk % KEY_CHUNK == 0
    cache_spec = pl.BlockSpec((None, None, past, A_V), lambda b, h, i: (b, layer, 0, h))
    full = lambda shape: pl.BlockSpec(shape, lambda b, h, i: (0,) * len(shape))
    return pl.pallas_call(
        body,
        out_shape=jax.ShapeDtypeStruct((nbatch * ns, GROUP_WIDTH), BF16),
        grid=(nbatch, A_HEADS, qpb),
        in_specs=[pl.BlockSpec((tq, A_V), lambda b, h, i: (qb0 + b * qpb + i, COL_AQ // A_V + h)),
                  pl.BlockSpec((ns, A_V), lambda b, h, i: (kb0 + b, COL_AK // A_V + h)),
                  pl.BlockSpec((ns, A_V), lambda b, h, i: (kb0 + b, COL_AV // A_V + h)),
                  cache_spec, cache_spec,
                  pl.BlockSpec((tq, LANES), lambda b, h, i: (i, 0)),
                  pl.BlockSpec((tq, LANES), lambda b, h, i: (i, 0)),
                  full((ns, LANES)), full((ns, LANES)),
                  full(lam_l.shape), full(subln_l.shape)],
        out_specs=pl.BlockSpec((tq, A_V), lambda b, h, i: (b * qpb + i, h)),
        scratch_shapes=[pltpu.VMEM((nk, A_V), BF16), pltpu.VMEM((nk, A_V), BF16)],
        compiler_params=_cparams("parallel", "parallel", "arbitrary"),
        name="sample_diff_attn",
    )(proj, proj, proj, cache_k, cache_v, cos_t, sin_t, cos_t, sin_t, lam_l, subln_l)


def _sample_window_attn(proj, cache_k, cache_v, layer, cos_t, sin_t, sink_l,
                        tp, nbatch, ns):
    w = C_WINDOW
    past = cache_k.shape[2]
    nblk = ns // w
    qb0 = tp // w
    kb0 = tp // ns
    gw = C_GROUP * C_HD
    scale = C_HD ** -0.5

    def body(q_ref, k_ref, v_ref, ck_ref, cv_ref, cq_ref, sq_ref, cosk_ref, sink_ref,
             snk_ref, o_ref, kr_ref, vb_ref):
        kvh = pl.program_id(1)
        i = pl.program_id(2)

        @pl.when(i == 0)
        def _():
            kr_ref[...] = _rope(k_ref[...], cosk_ref[...], sink_ref[...], C_HD // 2).astype(BF16)
            vb_ref[...] = v_ref[...].astype(BF16)

        qs = [_rope(q_ref[:, g * C_HD:(g + 1) * C_HD], cq_ref[...], sq_ref[...], C_HD // 2)
              for g in range(C_GROUP)]
        qq = jnp.concatenate(qs, axis=0).astype(BF16)
        start = pl.multiple_of(jnp.clip((i - 1) * w, 0, ns - 3 * w), w)
        kw = kr_ref[pl.ds(start, 3 * w), :]
        vw = vb_ref[pl.ds(start, 3 * w), :]
        s_loc = _dot_nt(qq, kw) * scale
        rows = lax.broadcasted_iota(jnp.int32, s_loc.shape, 0)
        cols = lax.broadcasted_iota(jnp.int32, s_loc.shape, 1)
        qpos = i * w + (rows & (w - 1))
        kpos = start + cols
        s_loc = jnp.where(jnp.abs(kpos - qpos) <= w, s_loc, NEG)
        s_ctx = _dot_nt(qq, ck_ref[...].astype(BF16)) * scale
        rcol = lax.broadcasted_iota(jnp.int32, (C_GROUP * w, 1), 0)
        snk = jnp.zeros((C_GROUP * w, 1), F32)
        for g in range(C_GROUP):
            snk = jnp.where(rcol >= g * w, snk_ref[kvh * C_GROUP + g], snk)
        m = jnp.maximum(jnp.maximum(jnp.max(s_loc, axis=-1, keepdims=True),
                                    jnp.max(s_ctx, axis=-1, keepdims=True)), snk)
        e_loc = jnp.exp(s_loc - m)
        e_ctx = jnp.exp(s_ctx - m)
        den = (jnp.sum(e_loc, axis=-1, keepdims=True) + jnp.sum(e_ctx, axis=-1, keepdims=True)
               + jnp.exp(snk - m))
        o = (_dot((e_loc / den).astype(BF16), vw)
             + _dot((e_ctx / den).astype(BF16), cv_ref[...].astype(BF16)))
        for g in range(C_GROUP):
            o_ref[:, g * C_HD:(g + 1) * C_HD] = o[g * w:(g + 1) * w, :].astype(BF16)

    cache_spec = pl.BlockSpec((None, None, past, C_HD), lambda b, h, i: (b, layer, 0, h))
    full = lambda shape: pl.BlockSpec(shape, lambda b, h, i: (0,) * len(shape))
    return pl.pallas_call(
        body,
        out_shape=jax.ShapeDtypeStruct((nbatch * ns, GROUP_WIDTH), BF16),
        grid=(nbatch, C_KV_HEADS, nblk),
        in_specs=[pl.BlockSpec((w, gw), lambda b, h, i: (qb0 + b * nblk + i, COL_CQ // gw + h)),
                  pl.BlockSpec((ns, C_HD), lambda b, h, i: (kb0 + b, COL_CK // C_HD + h)),
                  pl.BlockSpec((ns, C_HD), lambda b, h, i: (kb0 + b, COL_CV // C_HD + h)),
                  cache_spec, cache_spec,
                  pl.BlockSpec((w, LANES), lambda b, h, i: (i, 0)),
                  pl.BlockSpec((w, LANES), lambda b, h, i: (i, 0)),
                  full((ns, LANES)), full((ns, LANES)),
                  pl.BlockSpec(memory_space=pltpu.SMEM)],
        out_specs=pl.BlockSpec((w, gw), lambda b, h, i: (b * nblk + i, h)),
        scratch_shapes=[pltpu.VMEM((ns, C_HD), BF16), pltpu.VMEM((ns, C_HD), BF16)],
        compiler_params=_cparams("parallel", "parallel", "arbitrary"),
        name="sample_window_attn",
    )(proj, proj, proj, cache_k, cache_v, cos_t, sin_t, cos_t, sin_t, sink_l)


def _sample_fourier(proj, w2, csh, csl, wf, tp, nbatch, ns, tr):
    rpb = ns // tr
    kb0 = tp // ns

    def body(u_ref, w2_ref, csh_ref, csl_ref, wf_ref, o_ref, t_ref):
        @pl.when(pl.program_id(1) == 0)
        def _():
            for g in range(B_GROUPS):
                sl = slice(g * B_GD, (g + 1) * B_GD)
                t_ref[:, sl] = _fourier_stage1(u_ref[:, sl], csh_ref[...], csl_ref[...]).astype(BF16)

        f = _dot(w2_ref[...], t_ref[...])
        o_ref[...] = _dot(f.astype(BF16), wf_ref[...]).astype(BF16)

    full = lambda shape: pl.BlockSpec(shape, lambda b, r: (0,) * len(shape))
    return pl.pallas_call(
        body,
        out_shape=jax.ShapeDtypeStruct((nbatch * ns, GROUP_WIDTH), BF16),
        grid=(nbatch, rpb),
        in_specs=[pl.BlockSpec((ns, GROUP_WIDTH), lambda b, r: (kb0 + b, COL_BU // GROUP_WIDTH)),
                  pl.BlockSpec((tr, 2 * ns), lambda b, r: (r, 0)),
                  full(csh.shape), full(csl.shape), full(wf.shape)],
        out_specs=pl.BlockSpec((tr, GROUP_WIDTH), lambda b, r: (b * rpb + r, 0)),
        scratch_shapes=[pltpu.VMEM((2 * ns, GROUP_WIDTH), BF16)],
        compiler_params=_cparams("parallel", "arbitrary"),
        name="sample_fourier",
    )(proj, w2, csh, csl, wf)


def _spatial_gate(proj, g, b, ws, bst, tr):
    t = proj.shape[0]

    def body(u_ref, v_ref, g_ref, b_ref, ws_ref, bs_ref, o_ref):
        vn = (_ln_rows(v_ref[...]) * g_ref[...] + b_ref[...]).astype(BF16)
        for c in range(tr // CHUNK):
            rs = slice(c * CHUNK, (c + 1) * CHUNK)
            for h in range(D_HEADS):
                cs = slice(h * D_HD, (h + 1) * D_HD)
                mixed = _dot(ws_ref[h], vn[rs, cs]) + bs_ref[:, h:h + 1]
                o_ref[rs, cs] = (u_ref[rs, cs] * mixed).astype(BF16)

    full = lambda shape: pl.BlockSpec(shape, lambda i: (0,) * len(shape))
    return pl.pallas_call(
        body,
        out_shape=jax.ShapeDtypeStruct((t, GROUP_WIDTH), BF16),
        grid=(t // tr,),
        in_specs=[pl.BlockSpec((tr, GROUP_WIDTH), lambda i: (i, COL_DU // GROUP_WIDTH)),
                  pl.BlockSpec((tr, GROUP_WIDTH), lambda i: (i, COL_DV // GROUP_WIDTH)),
                  full(g.shape), full(b.shape), full(ws.shape), full(bst.shape)],
        out_specs=pl.BlockSpec((tr, GROUP_WIDTH), lambda i: (i, 0)),
        compiler_params=_cparams("parallel"),
        name="spatial_gate",
    )(proj, proj, g, b, ws, bst)


def _post_ln(x, f, gate, g, b):
    return _ln_rows(ALPHA * x + gate * f) * g + b


def _out_proj(abc_p, abc_s, od, w, x, mods_l, g1, b1, router, tp, ns, tm):
    t, d = x.shape
    with_router = router is not None
    np_tiles = tp // tm

    def body(ap_ref, bp_ref, cp_ref, as_ref, bs_ref, cs_ref, od_ref, w_ref, x_ref, m_ref,
             g_ref, b_ref, *rest):
        if with_router:
            wrh_ref, wrl_ref, br_ref, x1_ref, h2_ref, lg_ref = rest
        else:
            x1_ref, h2_ref = rest
        gw = GROUP_WIDTH
        is_ctx = pl.program_id(0) < np_tiles
        mix = _dot(jnp.where(is_ctx, ap_ref[...], as_ref[...]), w_ref[0:gw, :])
        mix += _dot(jnp.where(is_ctx, bp_ref[...], bs_ref[...]), w_ref[gw:2 * gw, :])
        mix += _dot(jnp.where(is_ctx, cp_ref[...], cs_ref[...]), w_ref[2 * gw:3 * gw, :])
        mix += _dot(od_ref[...], w_ref[3 * gw:4 * gw, :])
        x1 = _post_ln(x_ref[...], mix, m_ref[0, 2:3, :], g_ref[...], b_ref[...])
        x1_ref[...] = x1
        h2 = _ln_rows(x1) * (1.0 + m_ref[0, 4:5, :]) + m_ref[0, 3:4, :]
        h2_ref[...] = h2.astype(h2_ref.dtype)
        if with_router:
            hh, hl = _split(h2)
            lg_ref[...] = _dot3(hh, hl, wrh_ref[...], wrl_ref[...]) + br_ref[...]

    row = functools.partial(_mod_row, tm=tm, tp=tp, ns=ns)
    full = lambda shape: pl.BlockSpec(shape, lambda i: (0,) * len(shape))
    tile = lambda n: pl.BlockSpec((tm, n), lambda i: (i, 0))
    ctx_tile = pl.BlockSpec((tm, GROUP_WIDTH), lambda i: (jnp.minimum(i, np_tiles - 1), 0))
    lat_tile = pl.BlockSpec((tm, GROUP_WIDTH), lambda i: (jnp.maximum(i - np_tiles, 0), 0))
    in_specs = [ctx_tile] * 3 + [lat_tile] * 3 + [
        tile(GROUP_WIDTH), _resident(w.shape), tile(d),
        pl.BlockSpec((1, 6, d), lambda i: (row(i), 0, 0)), full(g1.shape), full(b1.shape)]
    args = [*abc_p, *abc_s, od, w, x, mods_l, g1, b1]
    out_shape = [jax.ShapeDtypeStruct((t, d), F32),
                 jax.ShapeDtypeStruct((t, d), F32 if with_router else BF16)]
    out_specs = [tile(d), tile(d)]
    if with_router:
        in_specs += [full(r.shape) for r in router]
        args += list(router)
        out_shape.append(jax.ShapeDtypeStruct((t, LANES), F32))
        out_specs.append(tile(LANES))
    return pl.pallas_call(
        body, out_shape=tuple(out_shape), grid=(t // tm,),
        in_specs=in_specs, out_specs=tuple(out_specs),
        compiler_params=_cparams("parallel"),
        name="out_proj",
    )(*args)


def _ffn_up(h2, wg, wu, tm, tn):
    t, d = h2.shape
    dff = wg.shape[1]

    def body(h_ref, wg_ref, wu_ref, a_ref):
        h = h_ref[...]
        a_ref[...] = (_silu(_dot(h, wg_ref[...])) * _dot(h, wu_ref[...])).astype(BF16)

    return pl.pallas_call(
        body,
        out_shape=jax.ShapeDtypeStruct((t, dff), BF16),
        grid=(dff // tn, t // tm),
        in_specs=[pl.BlockSpec((tm, d), lambda j, i: (i, 0)),
                  pl.BlockSpec((d, tn), lambda j, i: (0, j)),
                  pl.BlockSpec((d, tn), lambda j, i: (0, j))],
        out_specs=pl.BlockSpec((tm, tn), lambda j, i: (i, j)),
        compiler_params=_cparams("parallel", "parallel"),
        name="ffn_up",
    )(h2, wg, wu)


def _ffn_down(a, wd, x1, mods_l, g2, b2, tp, ns, tm):
    t, d = x1.shape
    dff = a.shape[1]

    def body(a_ref, wd_ref, x_ref, m_ref, g_ref, b_ref, o_ref):
        f = _dot(a_ref[...], wd_ref[...])
        o_ref[...] = _post_ln(x_ref[...], f, m_ref[0, 5:6, :], g_ref[...], b_ref[...])

    row = functools.partial(_mod_row, tm=tm, tp=tp, ns=ns)
    full = lambda shape: pl.BlockSpec(shape, lambda i: (0,) * len(shape))
    return pl.pallas_call(
        body,
        out_shape=jax.ShapeDtypeStruct((t, d), F32),
        grid=(t // tm,),
        in_specs=[pl.BlockSpec((tm, dff), lambda i: (i, 0)),
                  _resident(wd.shape),
                  pl.BlockSpec((tm, d), lambda i: (i, 0)),
                  pl.BlockSpec((1, 6, d), lambda i: (row(i), 0, 0)),
                  full(g2.shape), full(b2.shape)],
        out_specs=pl.BlockSpec((tm, d), lambda i: (i, 0)),
        compiler_params=_cparams("parallel"),
        name="ffn_down",
    )(a, wd, x1, mods_l, g2, b2)


ROUTE_I1, ROUTE_I2, ROUTE_W1, ROUTE_W2, ROUTE_R1, ROUTE_R2 = range(6)


def _route(logits, tr):
    t = logits.shape[0]

    def body(lg_ref, route_ref, cnt_ref, carry_ref):
        @pl.when(pl.program_id(0) == 0)
        def _():
            carry_ref[...] = jnp.zeros_like(carry_ref)

        lane = lax.broadcasted_iota(jnp.int32, (tr, LANES), 1)
        lane_f = lane.astype(F32)
        lg = jnp.where(lane < N_EXPERTS, lg_ref[...], -jnp.inf)
        m1 = jnp.max(lg, axis=-1, keepdims=True)
        i1 = jnp.min(jnp.where(lg == m1, lane_f, float(LANES)), axis=-1, keepdims=True)
        lg2 = jnp.where(lane_f == i1, -jnp.inf, lg)
        m2 = jnp.max(lg2, axis=-1, keepdims=True)
        i2 = jnp.min(jnp.where(lg2 == m2, lane_f, float(LANES)), axis=-1, keepdims=True)
        e2 = jnp.exp(m2 - m1)
        den = 1.0 + e2
        oh1 = lane_f == i1
        oh2 = lane_f == i2
        sel = jnp.where(oh1 | oh2, 1.0, 0.0)
        ri = lax.broadcasted_iota(jnp.int32, (tr, tr), 0)
        ci = lax.broadcasted_iota(jnp.int32, (tr, tr), 1)
        tri = jnp.where(ri > ci, 1.0, 0.0).astype(BF16)
        prefix = _dot(tri, sel.astype(BF16)) + carry_ref[0:1, :]
        r1 = jnp.sum(jnp.where(oh1, prefix, 0.0), axis=-1, keepdims=True)
        r2 = jnp.sum(jnp.where(oh2, prefix, 0.0), axis=-1, keepdims=True)
        carry_ref[0:1, :] = carry_ref[0:1, :] + jnp.sum(sel, axis=0, keepdims=True)
        out = jnp.zeros((tr, LANES), F32)
        for idx, val in ((ROUTE_I1, i1), (ROUTE_I2, i2), (ROUTE_W1, 1.0 / den),
                         (ROUTE_W2, e2 / den), (ROUTE_R1, r1), (ROUTE_R2, r2)):
            out = jnp.where(lane == idx, val, out)
        route_ref[...] = out
        cnt_ref[...] = carry_ref[...]

    return pl.pallas_call(
        body,
        out_shape=(jax.ShapeDtypeStruct((t, LANES), F32), jax.ShapeDtypeStruct((8, LANES), F32)),
        grid=(t // tr,),
        in_specs=[pl.BlockSpec((tr, LANES), lambda i: (i, 0))],
        out_specs=(pl.BlockSpec((tr, LANES), lambda i: (i, 0)),
                   pl.BlockSpec((8, LANES), lambda i: (0, 0))),
        scratch_shapes=[pltpu.VMEM((8, LANES), F32)],
        compiler_params=_cparams("arbitrary"),
        name="route",
    )(logits)


ROW_COPY_UNROLL = 8


def _experts(tile_expert, n_used, src, h, wg, wu, wd, tm, ck):
    d = h.shape[1]
    nt = src.shape[0] // tm
    dff = wg.shape[2]

    def row_copy(src_ref, h_ref, xg_ref, sem, tile, slot, r):
        return pltpu.make_async_copy(h_ref.at[pl.ds(src_ref[tile * tm + r], 1)],
                                     xg_ref.at[slot, pl.ds(r, 1)], sem.at[slot])

    def body(te_ref, nu_ref, src_ref, h_ref, wg_ref, wu_ref, wd_ref, y_ref, xg_ref, a_ref, sem):
        del te_ref
        i = pl.program_id(0)
        nu = nu_ref[0]

        def issue(tile, slot):
            def step(r, carry):
                row_copy(src_ref, h_ref, xg_ref, sem, tile, slot, r).start()
                return carry
            lax.fori_loop(0, tm, step, 0, unroll=ROW_COPY_UNROLL)

        def drain(tile, slot):
            def step(r, carry):
                row_copy(src_ref, h_ref, xg_ref, sem, tile, slot, r).wait()
                return carry
            lax.fori_loop(0, tm, step, 0, unroll=ROW_COPY_UNROLL)

        @pl.when(i == 0)
        def _():
            issue(0, 0)

        @pl.when(i < nu)
        def _():
            slot = i & 1
            drain(i, slot)

            @pl.when(i + 1 < nu)
            def _():
                issue(i + 1, 1 - slot)

            x = xg_ref[slot].astype(BF16)
            for c in range(dff // ck):
                cs = slice(c * ck, (c + 1) * ck)
                a_ref[:, cs] = (_silu(_dot(x, wg_ref[:, cs])) * _dot(x, wu_ref[:, cs])).astype(BF16)
            y_ref[...] = _dot(a_ref[...], wd_ref[...])

        @pl.when(i >= nu)
        def _():
            y_ref[...] = jnp.zeros_like(y_ref)

    def expert_weights(shape):
        return pl.BlockSpec((None,) + shape, lambda i, te, nu, src: (te[i], 0, 0),
                            pipeline_mode=pl.Buffered(1))

    return pl.pallas_call(
        body,
        out_shape=jax.ShapeDtypeStruct((nt * tm, d), F32),
        grid_spec=pltpu.PrefetchScalarGridSpec(
            num_scalar_prefetch=3, grid=(nt,),
            in_specs=[pl.BlockSpec(memory_space=pl.ANY),
                      expert_weights((d, dff)), expert_weights((d, dff)),
                      expert_weights((dff, d))],
            out_specs=pl.BlockSpec((tm, d), lambda i, te, nu, src: (i, 0)),
            scratch_shapes=[pltpu.VMEM((2, tm, d), F32), pltpu.VMEM((tm, dff), BF16),
                            pltpu.SemaphoreType.DMA((2,))]),
        compiler_params=_cparams("arbitrary"),
        name="experts",
    )(tile_expert, n_used, src, h, wg, wu, wd)


def _combine(pos1, pos2, ys, route, x1, mods_l, g2, b2, tp, ns, tc):
    t, d = x1.shape

    def row_copies(p1_ref, p2_ref, ys_ref, y1_ref, y2_ref, sem, tile, slot, r):
        tok = tile * tc + r
        return (pltpu.make_async_copy(ys_ref.at[pl.ds(p1_ref[tok], 1)],
                                      y1_ref.at[slot, pl.ds(r, 1)], sem.at[0, slot]),
                pltpu.make_async_copy(ys_ref.at[pl.ds(p2_ref[tok], 1)],
                                      y2_ref.at[slot, pl.ds(r, 1)], sem.at[1, slot]))

    def body(p1_ref, p2_ref, ys_ref, rt_ref, x_ref, m_ref, g_ref, b_ref, o_ref,
             y1_ref, y2_ref, sem):
        j = pl.program_id(0)

        def issue(tile, slot):
            def step(r, carry):
                for cp in row_copies(p1_ref, p2_ref, ys_ref, y1_ref, y2_ref, sem, tile, slot, r):
                    cp.start()
                return carry
            lax.fori_loop(0, tc, step, 0, unroll=ROW_COPY_UNROLL)

        def drain(tile, slot):
            def step(r, carry):
                for cp in row_copies(p1_ref, p2_ref, ys_ref, y1_ref, y2_ref, sem, tile, slot, r):
                    cp.wait()
                return carry
            lax.fori_loop(0, tc, step, 0, unroll=ROW_COPY_UNROLL)

        @pl.when(j == 0)
        def _():
            issue(0, 0)

        slot = j & 1
        drain(j, slot)

        @pl.when(j + 1 < pl.num_programs(0))
        def _():
            issue(j + 1, 1 - slot)

        rt = rt_ref[...]
        f = (rt[:, ROUTE_W1:ROUTE_W1 + 1] * y1_ref[slot]
             + rt[:, ROUTE_W2:ROUTE_W2 + 1] * y2_ref[slot])
        o_ref[...] = _post_ln(x_ref[...], f, m_ref[0, 5:6, :], g_ref[...], b_ref[...])

    row = functools.partial(_mod_row, tm=tc, tp=tp, ns=ns)
    full = lambda shape: pl.BlockSpec(shape, lambda i, p1, p2: (0,) * len(shape))
    return pl.pallas_call(
        body,
        out_shape=jax.ShapeDtypeStruct((t, d), F32),
        grid_spec=pltpu.PrefetchScalarGridSpec(
            num_scalar_prefetch=2, grid=(t // tc,),
            in_specs=[pl.BlockSpec(memory_space=pl.ANY),
                      pl.BlockSpec((tc, LANES), lambda i, p1, p2: (i, 0)),
                      pl.BlockSpec((tc, d), lambda i, p1, p2: (i, 0)),
                      pl.BlockSpec((1, 6, d), lambda i, p1, p2: (row(i), 0, 0)),
                      full(g2.shape), full(b2.shape)],
            out_specs=pl.BlockSpec((tc, d), lambda i, p1, p2: (i, 0)),
            scratch_shapes=[pltpu.VMEM((2, tc, d), F32), pltpu.VMEM((2, tc, d), F32),
                            pltpu.SemaphoreType.DMA((2, 2))]),
        compiler_params=_cparams("arbitrary"),
        name="combine",
    )(pos1, pos2, ys, route, x1, mods_l, g2, b2)


def _ffn_routed(h2, logits, wg, wu, wd, x1, mods_l, g2, b2, tp, ns, tm, ck):
    t = h2.shape[0]
    route, cnt = _route(logits, min(512, t))
    i1 = route[:, ROUTE_I1].astype(jnp.int32)
    i2 = route[:, ROUTE_I2].astype(jnp.int32)
    counts = cnt[0, :N_EXPERTS].astype(jnp.int32)
    padded = (counts + tm - 1) // tm * tm
    ends = jnp.cumsum(padded)
    starts = ends - padded
    pos1 = starts[i1] + route[:, ROUTE_R1].astype(jnp.int32)
    pos2 = starts[i2] + route[:, ROUTE_R2].astype(jnp.int32)
    nt = 2 * t // tm + N_EXPERTS
    n_used = ends[-1] // tm
    tile_start = jnp.arange(nt, dtype=jnp.int32) * tm
    tile_expert = jnp.sum(tile_start[:, None] >= ends[None, :], axis=1).astype(jnp.int32)
    tile_expert = jnp.minimum(tile_expert, tile_expert[n_used - 1])
    tok = jnp.arange(t, dtype=jnp.int32)
    src = jnp.zeros((nt * tm,), jnp.int32).at[pos1].set(tok).at[pos2].set(tok)

    ys = _experts(tile_expert, n_used.reshape(1), src, h2, wg, wu, wd, tm, ck)
    return _combine(pos1, pos2, ys, route, x1, mods_l, g2, b2, tp, ns, tm)


def _permute_in_cols(w):
    d = w.shape[1]
    qk = w[..., :4 * A_HEADS * A_QK].reshape(w.shape[0], d, 2, 2, A_HEADS, A_QK)
    qk = qk.transpose(0, 1, 2, 4, 3, 5).reshape(w.shape[0], d, 4 * A_HEADS * A_QK)
    return jnp.concatenate([qk, w[..., 4 * A_HEADS * A_QK:]], axis=-1)


def kernel(x_prompt, x_sample, cache_a_k, cache_a_v, cache_c_k, cache_c_v, c, c_ctx, w_ada, b_ada, w_in, lam, a_subln, w_fnet, c_sink, d_norm_g, d_norm_b, d_ws, d_bs, w_out, ln1_g, ln1_b, ln2_g, ln2_b, ffn_wg, ffn_wu, ffn_wd, w_router, b_router, moe_wg, moe_wu, moe_wd):
    nb_p, n_p, d = x_prompt.shape
    nb_s, n_s, _ = x_sample.shape
    depth = w_in.shape[0]
    past = cache_a_k.shape[2]
    tp = nb_p * n_p
    t = tp + nb_s * n_s
    assert tp % n_s == 0 and n_s % 512 == 0 and n_p % CHUNK == 0 and nb_s + 1 <= 8

    x = jnp.concatenate([x_prompt.reshape(tp, d), x_sample.reshape(nb_s * n_s, d)], axis=0)
    c_all = jnp.zeros((8, d), F32).at[0].set(c_ctx).at[1:1 + nb_s].set(c)
    mods = _modulation(c_all, w_ada, b_ada).reshape(depth, 8, 6, d)

    w_in_b = _permute_in_cols(w_in.astype(BF16))
    w_out_b = w_out.astype(BF16)
    w_fnet_b = w_fnet.astype(BF16)
    d_ws_b = d_ws.astype(BF16)
    rope_a = _rope_tables(n_s, A_QK)
    rope_c = _rope_tables(n_s, C_HD)
    csh, csl = _dft_channel_tables()
    w2p_h, w2p_l = _dft_tables(n_p)
    w2s_h, _ = _dft_tables(n_s)
    ca_k = cache_a_k.reshape(nb_s, depth, past, A_HEADS * A_V)
    ca_v = cache_a_v.reshape(nb_s, depth, past, A_HEADS * A_V)
    cc_k = cache_c_k.reshape(nb_s, depth, past, C_KV_HEADS * C_HD)
    cc_v = cache_c_v.reshape(nb_s, depth, past, C_KV_HEADS * C_HD)

    tm_big = math.gcd(1024, tp, n_s)
    tm_mid = math.gcd(512, tp, n_s)
    tm_small = math.gcd(256, tp, n_s)
    new_ak, new_av, new_ck, new_cv = [], [], [], []
    for l in range(depth):
        lam_init = 0.8 - 0.6 * math.exp(-0.3 * l)
        proj = _in_proj(x, mods[l], w_in_b[l], tp, n_s, tm_big, 1024)
        new_ak.append(proj[:tp, COL_AK:COL_AV].reshape(nb_p, n_p, A_HEADS, 2 * A_QK))
        new_av.append(proj[:tp, COL_AV:COL_BU].reshape(nb_p, n_p, A_HEADS, A_V))
        new_ck.append(proj[:tp, COL_CK:COL_CV].reshape(nb_p, n_p, C_KV_HEADS, C_HD))
        new_cv.append(proj[:tp, COL_CV:COL_DU].reshape(nb_p, n_p, C_KV_HEADS, C_HD))

        subln = a_subln[l].reshape(1, A_V)
        abc_p = _prompt_mixers(proj, lam[l], subln, c_sink[l], w2p_h, w2p_l, csh, csl,
                               w_fnet_b[l], lam_init, nb_p, n_p)
        oa_s = _sample_diff_attn(proj, ca_k, ca_v, l, rope_a[0], rope_a[1], lam[l], subln,
                                 lam_init, tp, nb_s, n_s, 256)
        oc_s = _sample_window_attn(proj, cc_k, cc_v, l, rope_c[0], rope_c[1], c_sink[l],
                                   tp, nb_s, n_s)
        ob_s = _sample_fourier(proj, w2s_h, csh, csl, w_fnet_b[l], tp, nb_s, n_s, tm_mid)
        abc_s = (oa_s, ob_s, oc_s)
        od = _spatial_gate(proj, d_norm_g[l].reshape(1, -1), d_norm_b[l].reshape(1, -1),
                           d_ws_b[l], d_bs[l].T, 256)

        g1, b1 = ln1_g[l].reshape(1, d), ln1_b[l].reshape(1, d)
        g2, b2 = ln2_g[l].reshape(1, d), ln2_b[l].reshape(1, d)
        j = l // 2
        if l % 2 == 0:
            x1, h2 = _out_proj(abc_p, abc_s, od, w_out_b[l], x, mods[l], g1, b1, None,
                               tp, n_s, tm_mid)
            a = _ffn_up(h2, ffn_wg[j].astype(BF16), ffn_wu[j].astype(BF16), tm_big, 512)
            x = _ffn_down(a, ffn_wd[j].astype(BF16), x1, mods[l], g2, b2, tp, n_s, tm_small)
        else:
            wr = jnp.zeros((d, LANES), F32).at[:, :N_EXPERTS].set(w_router[j])
            br = jnp.zeros((1, LANES), F32).at[0, :N_EXPERTS].set(b_router[j])
            router = (*_split(wr), br)
            x1, h2, logits = _out_proj(abc_p, abc_s, od, w_out_b[l], x, mods[l], g1, b1, router,
                                       tp, n_s, tm_mid)
            x = _ffn_routed(h2, logits, moe_wg[j].astype(BF16), moe_wu[j].astype(BF16),
                            moe_wd[j].astype(BF16), x1, mods[l], g2, b2, tp, n_s, tm_small, 256)

    y_p = x[:tp].reshape(nb_p, n_p, d)
    y_s = x[tp:].reshape(nb_s, n_s, d)
    return (y_p, y_s, jnp.stack(new_ak, axis=1), jnp.stack(new_av, axis=1),
            jnp.stack(new_ck, axis=1), jnp.stack(new_cv, axis=1))
```

```python
import functools
import math

import jax
import jax.numpy as jnp
import numpy as np
from jax import lax
from jax.experimental import pallas as pl
from jax.experimental.pallas import tpu as pltpu

F32 = jnp.float32
BF16 = jnp.bfloat16

GRID_W = 64
A_V = 128
A_QK = 64
A_HEADS = 4
B_GD = 128
B_GROUPS = 4
C_HD = 128
C_HEADS = 4
C_KV_HEADS = 2
C_GROUP = C_HEADS // C_KV_HEADS
C_WINDOW = 128
D_HD = 128
D_HEADS = 4
CHUNK = 128
GROUP_WIDTH = 512
N_EXPERTS = 8
ROPE_BASE = 10000.0
LN_EPS = 1e-6
SUBLN_EPS = 1e-5
NEG = -1e30
TOTAL_DEPTH = 4
ALPHA = (2.0 * TOTAL_DEPTH) ** 0.25

COL_AQ = 0
COL_AK = 512
COL_AV = 1024
COL_BU = 1536
COL_CQ = 2048
COL_CK = 2560
COL_CV = 2816
COL_DU = 3072
COL_DV = 3584

LANES = 128
V7X_VMEM_BYTES = 64 * 1024 * 1024
VMEM_LIMIT = V7X_VMEM_BYTES - 8 * 1024 * 1024


def _cparams(*sem):
    return pltpu.CompilerParams(dimension_semantics=sem, vmem_limit_bytes=VMEM_LIMIT)


def _dot(a, b):
    return jnp.dot(a, b, preferred_element_type=F32)


def _dot_nt(a, b):
    return lax.dot_general(a, b, (((1,), (1,)), ((), ())), preferred_element_type=F32)


def _split(a):
    hi = a.astype(BF16)
    lo = (a - hi.astype(F32)).astype(BF16)
    return hi, lo


def _split3(a):
    a0 = a.astype(BF16)
    r1 = a - a0.astype(F32)
    a1 = r1.astype(BF16)
    a2 = (r1 - a1.astype(F32)).astype(BF16)
    return a0, a1, a2


def _dot3(a_hi, a_lo, b_hi, b_lo):
    return _dot(a_hi, b_hi) + (_dot(a_hi, b_lo) + _dot(a_lo, b_hi))


def _ln_rows(x):
    xc = x - jnp.mean(x, axis=-1, keepdims=True)
    return xc * lax.rsqrt(jnp.mean(xc * xc, axis=-1, keepdims=True) + LN_EPS)


def _silu(x):
    return x / (1.0 + jnp.exp(-x))


def _mod_row(i, tm, tp, ns):
    return jnp.where(i * tm < tp, 0, 1 + (i * tm - tp) // ns)


def _modulation(c_all, w_ada, b_ada):
    depth, d, n6 = w_ada.shape
    rows = c_all.shape[0]
    tn = 512

    def body(c_ref, w_ref, b_ref, o_ref):
        s = _silu(c_ref[...]).astype(BF16)
        o_ref[0] = _dot(s, w_ref[0].astype(BF16)) + b_ref[0]

    return pl.pallas_call(
        body,
        out_shape=jax.ShapeDtypeStruct((depth, rows, n6), F32),
        grid=(depth, n6 // tn),
        in_specs=[pl.BlockSpec((rows, d), lambda l, j: (0, 0)),
                  pl.BlockSpec((1, d, tn), lambda l, j: (l, 0, j)),
                  pl.BlockSpec((1, 1, tn), lambda l, j: (l, 0, j))],
        out_specs=pl.BlockSpec((1, rows, tn), lambda l, j: (l, 0, j)),
        compiler_params=_cparams("parallel", "parallel"),
        name="modulation",
    )(c_all, w_ada, b_ada.reshape(depth, 1, n6))


def _mod_ln(x, m_ref, k):
    return _ln_rows(x) * (1.0 + m_ref[0, k + 1:k + 2, :]) + m_ref[0, k:k + 1, :]


def _first_ln(x_p, x_s, mods_l, ns, tm):
    tp, d = x_p.shape
    t = tp + x_s.shape[0]
    np_tiles = tp // tm

    def body(xp_ref, xs_ref, m_ref, x_ref, h_ref):
        x = jnp.where(pl.program_id(0) < np_tiles, xp_ref[...], xs_ref[...])
        x_ref[...] = x
        h_ref[...] = _mod_ln(x, m_ref, 0).astype(BF16)

    row = functools.partial(_mod_row, tm=tm, tp=tp, ns=ns)
    tile = pl.BlockSpec((tm, d), lambda i: (i, 0))
    return pl.pallas_call(
        body,
        out_shape=(jax.ShapeDtypeStruct((t, d), F32), jax.ShapeDtypeStruct((t, d), BF16)),
        grid=(t // tm,),
        in_specs=[pl.BlockSpec((tm, d), lambda i: (jnp.minimum(i, np_tiles - 1), 0)),
                  pl.BlockSpec((tm, d), lambda i: (jnp.maximum(i - np_tiles, 0), 0)),
                  pl.BlockSpec((1, 6, d), lambda i: (row(i), 0, 0))],
        out_specs=(tile, tile),
        compiler_params=_cparams("parallel"),
        name="first_ln",
    )(x_p, x_s, mods_l)


IN_TN = 1024
CACHE_CUTS = ((COL_AK // IN_TN, COL_AK % IN_TN, COL_AV - COL_AK),
              (COL_AV // IN_TN, COL_AV % IN_TN, COL_BU - COL_AV),
              (COL_CK // IN_TN, COL_CK % IN_TN, COL_CV - COL_CK),
              (COL_CV // IN_TN, COL_CV % IN_TN, COL_DU - COL_CV))


def _in_proj(h, w_all, layer, tp, tm):
    t, d = h.shape
    n = w_all.shape[2]
    tn = IN_TN
    np_tiles = tp // tm

    def body(h_ref, w_ref, o_ref, ak_ref, av_ref, ck_ref, cv_ref):
        i = pl.program_id(0)
        j = pl.program_id(1)
        o_ref[...] = _dot(h_ref[...], w_ref[...])

        for c_ref, (jt, c0, width) in zip((ak_ref, av_ref, ck_ref, cv_ref), CACHE_CUTS):
            @pl.when(jnp.logical_and(j == jt, i < np_tiles))
            def _(c_ref=c_ref, c0=c0, width=width):
                c_ref[...] = o_ref[:, c0:c0 + width]

    cache_spec = lambda width: pl.BlockSpec((tm, width), lambda i, j: (jnp.minimum(i, np_tiles - 1), 0))
    widths = [cut[2] for cut in CACHE_CUTS]
    return pl.pallas_call(
        body,
        out_shape=(jax.ShapeDtypeStruct((t, n), F32),
                   *[jax.ShapeDtypeStruct((tp, wd), F32) for wd in widths]),
        grid=(t // tm, n // tn),
        in_specs=[pl.BlockSpec((tm, d), lambda i, j: (i, 0)),
                  pl.BlockSpec((None, d, tn), lambda i, j: (layer, 0, j))],
        out_specs=(pl.BlockSpec((tm, tn), lambda i, j: (i, j)), *[cache_spec(wd) for wd in widths]),
        compiler_params=_cparams("parallel", "arbitrary"),
        name="in_proj",
    )(h, w_all)


def _lam_value(lam_ref, lam_init):
    lv = lam_ref[...]
    d1 = jnp.sum(lv[0:1] * lv[1:2], axis=-1, keepdims=True)
    d2 = jnp.sum(lv[2:3] * lv[3:4], axis=-1, keepdims=True)
    return jnp.exp(d1) - jnp.exp(d2) + lam_init


def _softmax(s):
    e = jnp.exp(s - jnp.max(s, axis=-1, keepdims=True))
    return e / jnp.sum(e, axis=-1, keepdims=True)


def _diff_attend(q, kb, vb, lam):
    lane = lax.broadcasted_iota(jnp.int32, q.shape, 1)
    q1 = jnp.where(lane < A_QK, q, 0.0).astype(BF16)
    q2 = jnp.where(lane >= A_QK, q, 0.0).astype(BF16)
    scale = A_QK ** -0.5
    s1 = _dot_nt(q1, kb) * scale
    s2 = _dot_nt(q2, kb) * scale
    p = _softmax(s1) - lam * _softmax(s2)
    return _dot(p.astype(BF16), vb)


KEY_CHUNK = 256


def _diff_attend_chunked(q, kr_ref, vb_ref, lam, kc):
    lane = lax.broadcasted_iota(jnp.int32, q.shape, 1)
    qs = q * (A_QK ** -0.5)
    q1 = jnp.where(lane < A_QK, qs, 0.0).astype(BF16)
    q2 = jnp.where(lane >= A_QK, qs, 0.0).astype(BF16)
    chunks = [slice(c * kc, (c + 1) * kc) for c in range(kr_ref.shape[0] // kc)]

    def fold(x, op):
        out = x[:, :LANES]
        for j in range(1, kc // LANES):
            out = op(out, x[:, j * LANES:(j + 1) * LANES])
        return out

    m1 = m2 = None
    for cs in chunks:
        kb = kr_ref[cs, :]
        c1 = fold(_dot_nt(q1, kb), jnp.maximum)
        c2 = fold(_dot_nt(q2, kb), jnp.maximum)
        m1 = c1 if m1 is None else jnp.maximum(m1, c1)
        m2 = c2 if m2 is None else jnp.maximum(m2, c2)
    m1 = jnp.max(m1, axis=-1, keepdims=True)
    m2 = jnp.max(m2, axis=-1, keepdims=True)
    l1 = l2 = o1 = o2 = 0.0
    for cs in chunks:
        kb = kr_ref[cs, :]
        vb = vb_ref[cs, :]
        e1 = jnp.exp(_dot_nt(q1, kb) - m1)
        e2 = jnp.exp(_dot_nt(q2, kb) - m2)
        l1 = l1 + fold(e1, jnp.add)
        l2 = l2 + fold(e2, jnp.add)
        o1 = o1 + _dot(e1.astype(BF16), vb)
        o2 = o2 + _dot(e2.astype(BF16), vb)
    l1 = jnp.sum(l1, axis=-1, keepdims=True)
    l2 = jnp.sum(l2, axis=-1, keepdims=True)
    return o1 / l1 - lam * (o2 / l2)


def _sub_ln(o, g, lam_init):
    r = lax.rsqrt(jnp.mean(o * o, axis=-1, keepdims=True) + SUBLN_EPS)
    return (o * r) * g * (1.0 - lam_init)


def _rope(x, cos, sin_signed, half):
    if 2 * half == LANES:
        swapped = pltpu.roll(x, half, axis=1)
    else:
        lane = lax.broadcasted_iota(jnp.int32, x.shape, 1)
        first = (lane % (2 * half)) < half
        swapped = jnp.where(first, pltpu.roll(x, LANES - half, axis=1),
                            pltpu.roll(x, half, axis=1))
    return x * cos + swapped * sin_signed


def _rope_tables(n_tokens, head_dim):
    rows = n_tokens // GRID_W
    row = np.repeat(np.arange(rows, dtype=np.float64), GRID_W)
    col = np.tile(np.arange(GRID_W, dtype=np.float64), rows)
    n_freq = head_dim // 4
    inv = np.power(ROPE_BASE, -np.arange(n_freq, dtype=np.float64) / n_freq)
    ang = np.concatenate([row[:, None] * inv, col[:, None] * inv], axis=-1)
    cos, sin = np.cos(ang), np.sin(ang)
    reps = LANES // head_dim
    cos_t = np.tile(np.concatenate([cos, cos], axis=-1), (1, reps))
    sin_t = np.tile(np.concatenate([-sin, sin], axis=-1), (1, reps))
    return cos_t.astype(np.float32), sin_t.astype(np.float32)


def _dft_cos_sin(n):
    jk = np.outer(np.arange(n), np.arange(n)) % n
    ang = jk.astype(np.float64) * (2.0 * math.pi / n)
    return np.cos(ang) * n ** -0.5, np.sin(ang) * n ** -0.5


def _dft_table(n):
    c, s = _dft_cos_sin(n)
    return np.concatenate([c, -s], axis=1).astype(np.float32)


def _dft_channel_table():
    c, s = _dft_cos_sin(B_GD)
    return np.concatenate([c, s], axis=1).astype(np.float32)


def _fourier_stage1(u, csh, csl):
    uh, ul = _split(u)
    t = _dot3(uh, ul, csh, csl)
    return jnp.concatenate([t[:, :B_GD], t[:, B_GD:]], axis=0)


PROMPT_BATCHES_PER_STEP = 2


def _prompt_mixers(proj, lam_l, subln_l, sink_l, w2, csh, csl, wf, lam_init, nbatch, n):
    t = nbatch * n
    bps = math.gcd(PROMPT_BATCHES_PER_STEP, nbatch)

    def body(p_ref, lam_ref, g_ref, sink_ref, w2_ref, csh_ref, csl_ref, wf_ref,
             oa_ref, ob_ref, oc_ref):
        lam = _lam_value(lam_ref, lam_init)
        scale = C_HD ** -0.5
        for bb in range(bps):
            rows = slice(bb * n, (bb + 1) * n)
            for h in range(A_HEADS):
                q = p_ref[rows, COL_AQ + h * A_V:COL_AQ + (h + 1) * A_V]
                kb = p_ref[rows, COL_AK + h * A_V:COL_AK + (h + 1) * A_V].astype(BF16)
                vb = p_ref[rows, COL_AV + h * A_V:COL_AV + (h + 1) * A_V].astype(BF16)
                o = _diff_attend(q, kb, vb, lam)
                oa_ref[rows, h * A_V:(h + 1) * A_V] = _sub_ln(o, g_ref[...], lam_init).astype(BF16)

            acc = None
            for g in range(B_GROUPS):
                u = p_ref[rows, COL_BU + g * B_GD:COL_BU + (g + 1) * B_GD]
                tb = _fourier_stage1(u, csh_ref[...], csl_ref[...]).astype(BF16)
                f = _dot(w2_ref[...], tb)
                part = _dot(f.astype(BF16), wf_ref[g * B_GD:(g + 1) * B_GD, :])
                acc = part if acc is None else acc + part
            ob_ref[rows, :] = acc.astype(BF16)

            for hq in range(C_HEADS):
                kvh = hq // C_GROUP
                q = p_ref[rows, COL_CQ + hq * C_HD:COL_CQ + (hq + 1) * C_HD].astype(BF16)
                kb = p_ref[rows, COL_CK + kvh * C_HD:COL_CK + (kvh + 1) * C_HD].astype(BF16)
                vb = p_ref[rows, COL_CV + kvh * C_HD:COL_CV + (kvh + 1) * C_HD].astype(BF16)
                s = _dot_nt(q, kb) * scale
                snk = sink_ref[hq]
                m = jnp.maximum(jnp.max(s, axis=-1, keepdims=True), snk)
                e = jnp.exp(s - m)
                den = jnp.sum(e, axis=-1, keepdims=True) + jnp.exp(snk - m)
                o = _dot((e / den).astype(BF16), vb)
                oc_ref[rows, hq * C_HD:(hq + 1) * C_HD] = o.astype(BF16)

    full = lambda shape: pl.BlockSpec(shape, lambda b: (0,) * len(shape))
    out = jax.ShapeDtypeStruct((t, GROUP_WIDTH), BF16)
    ospec = pl.BlockSpec((bps * n, GROUP_WIDTH), lambda b: (b, 0))
    return pl.pallas_call(
        body,
        out_shape=(out, out, out),
        grid=(nbatch // bps,),
        in_specs=[pl.BlockSpec((bps * n, proj.shape[1]), lambda b: (b, 0)),
                  full(lam_l.shape), full(subln_l.shape),
                  pl.BlockSpec(memory_space=pltpu.SMEM),
                  full(w2.shape), full(csh.shape), full(csl.shape), full(wf.shape)],
        out_specs=(ospec, ospec, ospec),
        compiler_params=_cparams("parallel"),
        name="prompt_mixers",
    )(proj, lam_l, subln_l, sink_l, w2, csh, csl, wf)


def _sample_diff_attn(proj, cache_k, cache_v, layer, cos_t, sin_t, lam_l, subln_l,
                      lam_init, tp, nbatch, ns, tq):
    past = cache_k.shape[2]
    nk = ns + past
    qb0 = tp // tq
    qpb = ns // tq
    kb0 = tp // ns

    def body(q_ref, k_ref, v_ref, ck_ref, cv_ref, cq_ref, sq_ref, cosk_ref, sink_ref,
             lam_ref, g_ref, o_ref, kr_ref, vb_ref):
        @pl.when(pl.program_id(2) == 0)
        def _():
            kr_ref[0:ns, :] = _rope(k_ref[...], cosk_ref[...], sink_ref[...], A_QK // 2).astype(BF16)
            kr_ref[ns:nk, :] = ck_ref[...].astype(BF16)
            vb_ref[0:ns, :] = v_ref[...].astype(BF16)
            vb_ref[ns:nk, :] = cv_ref[...].astype(BF16)

        lam = _lam_value(lam_ref, lam_init)
        q = _rope(q_ref[...], cq_ref[...], sq_ref[...], A_QK // 2)
        o = _diff_attend_chunked(q, kr_ref, vb_ref, lam, KEY_CHUNK)
        o_ref[...] = _sub_ln(o, g_ref[...], lam_init).astype(BF16)

    assert nk % KEY_CHUNK == 0
    cache_spec = pl.BlockSpec((None, None, past, A_V), lambda b, h, i: (b, layer, 0, h))
    full = lambda shape: pl.BlockSpec(shape, lambda b, h, i: (0,) * len(shape))
    return pl.pallas_call(
        body,
        out_shape=jax.ShapeDtypeStruct((nbatch * ns, GROUP_WIDTH), BF16),
        grid=(nbatch, A_HEADS, qpb),
        in_specs=[pl.BlockSpec((tq, A_V), lambda b, h, i: (qb0 + b * qpb + i, COL_AQ // A_V + h)),
                  pl.BlockSpec((ns, A_V), lambda b, h, i: (kb0 + b, COL_AK // A_V + h)),
                  pl.BlockSpec((ns, A_V), lambda b, h, i: (kb0 + b, COL_AV // A_V + h)),
                  cache_spec, cache_spec,
                  pl.BlockSpec((tq, LANES), lambda b, h, i: (i, 0)),
                  pl.BlockSpec((tq, LANES), lambda b, h, i: (i, 0)),
                  full((ns, LANES)), full((ns, LANES)),
                  full(lam_l.shape), full(subln_l.shape)],
        out_specs=pl.BlockSpec((tq, A_V), lambda b, h, i: (b * qpb + i, h)),
        scratch_shapes=[pltpu.VMEM((nk, A_V), BF16), pltpu.VMEM((nk, A_V), BF16)],
        compiler_params=_cparams("parallel", "parallel", "arbitrary"),
        name="sample_diff_attn",
    )(proj, proj, proj, cache_k, cache_v, cos_t, sin_t, cos_t, sin_t, lam_l, subln_l)


def _sample_window_attn(proj, cache_k, cache_v, layer, cos_t, sin_t, sink_l,
                        tp, nbatch, ns):
    w = C_WINDOW
    past = cache_k.shape[2]
    nblk = ns // w
    qb0 = tp // w
    kb0 = tp // ns
    gw = C_GROUP * C_HD
    scale = C_HD ** -0.5

    def body(q_ref, k_ref, v_ref, ck_ref, cv_ref, cq_ref, sq_ref, cosk_ref, sink_ref,
             snk_ref, o_ref, kr_ref, vb_ref):
        kvh = pl.program_id(1)
        i = pl.program_id(2)

        @pl.when(i == 0)
        def _():
            kr_ref[...] = _rope(k_ref[...], cosk_ref[...], sink_ref[...], C_HD // 2).astype(BF16)
            vb_ref[...] = v_ref[...].astype(BF16)

        qs = [_rope(q_ref[:, g * C_HD:(g + 1) * C_HD], cq_ref[...], sq_ref[...], C_HD // 2)
              for g in range(C_GROUP)]
        qq = jnp.concatenate(qs, axis=0).astype(BF16)
        start = pl.multiple_of(jnp.clip((i - 1) * w, 0, ns - 3 * w), w)
        kw = kr_ref[pl.ds(start, 3 * w), :]
        vw = vb_ref[pl.ds(start, 3 * w), :]
        s_loc = _dot_nt(qq, kw) * scale
        rows = lax.broadcasted_iota(jnp.int32, s_loc.shape, 0)
        cols = lax.broadcasted_iota(jnp.int32, s_loc.shape, 1)
        qpos = i * w + (rows & (w - 1))
        kpos = start + cols
        s_loc = jnp.where(jnp.abs(kpos - qpos) <= w, s_loc, NEG)
        s_ctx = _dot_nt(qq, ck_ref[...].astype(BF16)) * scale
        rcol = lax.broadcasted_iota(jnp.int32, (C_GROUP * w, 1), 0)
        snk = jnp.zeros((C_GROUP * w, 1), F32)
        for g in range(C_GROUP):
            snk = jnp.where(rcol >= g * w, snk_ref[kvh * C_GROUP + g], snk)
        m = jnp.maximum(jnp.maximum(jnp.max(s_loc, axis=-1, keepdims=True),
                                    jnp.max(s_ctx, axis=-1, keepdims=True)), snk)
        e_loc = jnp.exp(s_loc - m)
        e_ctx = jnp.exp(s_ctx - m)
        den = (jnp.sum(e_loc, axis=-1, keepdims=True) + jnp.sum(e_ctx, axis=-1, keepdims=True)
               + jnp.exp(snk - m))
        o = (_dot((e_loc / den).astype(BF16), vw)
             + _dot((e_ctx / den).astype(BF16), cv_ref[...].astype(BF16)))
        for g in range(C_GROUP):
            o_ref[:, g * C_HD:(g + 1) * C_HD] = o[g * w:(g + 1) * w, :].astype(BF16)

    cache_spec = pl.BlockSpec((None, None, past, C_HD), lambda b, h, i: (b, layer, 0, h))
    full = lambda shape: pl.BlockSpec(shape, lambda b, h, i: (0,) * len(shape))
    return pl.pallas_call(
        body,
        out_shape=jax.ShapeDtypeStruct((nbatch * ns, GROUP_WIDTH), BF16),
        grid=(nbatch, C_KV_HEADS, nblk),
        in_specs=[pl.BlockSpec((w, gw), lambda b, h, i: (qb0 + b * nblk + i, COL_CQ // gw + h)),
                  pl.BlockSpec((ns, C_HD), lambda b, h, i: (kb0 + b, COL_CK // C_HD + h)),
                  pl.BlockSpec((ns, C_HD), lambda b, h, i: (kb0 + b, COL_CV // C_HD + h)),
                  cache_spec, cache_spec,
                  pl.BlockSpec((w, LANES), lambda b, h, i: (i, 0)),
                  pl.BlockSpec((w, LANES), lambda b, h, i: (i, 0)),
                  full((ns, LANES)), full((ns, LANES)),
                  pl.BlockSpec(memory_space=pltpu.SMEM)],
        out_specs=pl.BlockSpec((w, gw), lambda b, h, i: (b * nblk + i, h)),
        scratch_shapes=[pltpu.VMEM((ns, C_HD), BF16), pltpu.VMEM((ns, C_HD), BF16)],
        compiler_params=_cparams("parallel", "parallel", "arbitrary"),
        name="sample_window_attn",
    )(proj, proj, proj, cache_k, cache_v, cos_t, sin_t, cos_t, sin_t, sink_l)


def _sample_fourier(proj, w2, csh, csl, wf, tp, nbatch, ns, tr):
    rpb = ns // tr
    kb0 = tp // ns

    def body(u_ref, w2_ref, csh_ref, csl_ref, wf_ref, o_ref, t_ref):
        @pl.when(pl.program_id(1) == 0)
        def _():
            for g in range(B_GROUPS):
                sl = slice(g * B_GD, (g + 1) * B_GD)
                t_ref[:, sl] = _fourier_stage1(u_ref[:, sl], csh_ref[...], csl_ref[...]).astype(BF16)

        f = _dot(w2_ref[...], t_ref[...])
        o_ref[...] = _dot(f.astype(BF16), wf_ref[...]).astype(BF16)

    full = lambda shape: pl.BlockSpec(shape, lambda b, r: (0,) * len(shape))
    return pl.pallas_call(
        body,
        out_shape=jax.ShapeDtypeStruct((nbatch * ns, GROUP_WIDTH), BF16),
        grid=(nbatch, rpb),
        in_specs=[pl.BlockSpec((ns, GROUP_WIDTH), lambda b, r: (kb0 + b, COL_BU // GROUP_WIDTH)),
                  pl.BlockSpec((tr, 2 * ns), lambda b, r: (r, 0)),
                  full(csh.shape), full(csl.shape), full(wf.shape)],
        out_specs=pl.BlockSpec((tr, GROUP_WIDTH), lambda b, r: (b * rpb + r, 0)),
        scratch_shapes=[pltpu.VMEM((2 * ns, GROUP_WIDTH), BF16)],
        compiler_params=_cparams("parallel", "arbitrary"),
        name="sample_fourier",
    )(proj, w2, csh, csl, wf)


def _spatial_gate(proj, g, b, ws, bst, tr):
    t = proj.shape[0]

    def body(u_ref, v_ref, g_ref, b_ref, ws_ref, bs_ref, o_ref):
        vn = (_ln_rows(v_ref[...]) * g_ref[...] + b_ref[...]).astype(BF16)
        for c in range(tr // CHUNK):
            rs = slice(c * CHUNK, (c + 1) * CHUNK)
            for h in range(D_HEADS):
                cs = slice(h * D_HD, (h + 1) * D_HD)
                mixed = _dot(ws_ref[h], vn[rs, cs]) + bs_ref[:, h:h + 1]
                o_ref[rs, cs] = (u_ref[rs, cs] * mixed).astype(BF16)

    full = lambda shape: pl.BlockSpec(shape, lambda i: (0,) * len(shape))
    return pl.pallas_call(
        body,
        out_shape=jax.ShapeDtypeStruct((t, GROUP_WIDTH), BF16),
        grid=(t // tr,),
        in_specs=[pl.BlockSpec((tr, GROUP_WIDTH), lambda i: (i, COL_DU // GROUP_WIDTH)),
                  pl.BlockSpec((tr, GROUP_WIDTH), lambda i: (i, COL_DV // GROUP_WIDTH)),
                  full(g.shape), full(b.shape), full(ws.shape), full(bst.shape)],
        out_specs=pl.BlockSpec((tr, GROUP_WIDTH), lambda i: (i, 0)),
        compiler_params=_cparams("parallel"),
        name="spatial_gate",
    )(proj, proj, g, b, ws, bst)


def _post_ln(x, f, gate, g, b):
    return _ln_rows(ALPHA * x + gate * f) * g + b


OUT_SUB_ROWS = 256
ROUTER_TERM_LANES = 8


def _pack_router(w_router_l, b_router_l):
    d = w_router_l.shape[0]
    assert N_EXPERTS <= ROUTER_TERM_LANES
    wp = jnp.zeros((d, LANES), BF16)
    for k, term in enumerate(_split3(w_router_l)):
        wp = wp.at[:, k * ROUTER_TERM_LANES:k * ROUTER_TERM_LANES + N_EXPERTS].set(term)
    br = jnp.zeros((1, LANES), F32).at[0, :N_EXPERTS].set(b_router_l)
    return wp, br


def _out_proj(abc_p, abc_s, od, w_all, layer, x, mods_l, g1, b1, router, tp, ns, tm):
    t, d = x.shape
    with_router = router is not None
    np_tiles = tp // tm
    sub = min(OUT_SUB_ROWS, tm)

    def body(ap_ref, bp_ref, cp_ref, as_ref, bs_ref, cs_ref, od_ref, w_ref, x_ref, m_ref,
             g_ref, b_ref, *rest):
        if with_router:
            wr_ref, br_ref, x1_ref, h2_ref, lg_ref, cat_ref = rest
        else:
            x1_ref, h2_ref, cat_ref = rest
        gw = GROUP_WIDTH
        is_ctx = pl.program_id(0) < np_tiles
        cat_ref[:, 0:gw] = jnp.where(is_ctx, ap_ref[...], as_ref[...])
        cat_ref[:, gw:2 * gw] = jnp.where(is_ctx, bp_ref[...], bs_ref[...])
        cat_ref[:, 2 * gw:3 * gw] = jnp.where(is_ctx, cp_ref[...], cs_ref[...])
        cat_ref[:, 3 * gw:4 * gw] = od_ref[...]
        for r in range(tm // sub):
            rs = slice(r * sub, (r + 1) * sub)
            mix = _dot(cat_ref[rs, :], w_ref[...])
            x1 = _post_ln(x_ref[rs, :], mix, m_ref[0, 2:3, :], g_ref[...], b_ref[...])
            x1_ref[rs, :] = x1
            h2 = _mod_ln(x1, m_ref, 3)
            h2_ref[rs, :] = h2.astype(h2_ref.dtype)
            if with_router:
                h0, h1, h2b = _split3(h2)
                p0, p1, p2 = (_dot(hk, wr_ref[...]) for hk in (h0, h1, h2b))
                left = lambda p, k: pltpu.roll(p, LANES - k * ROUTER_TERM_LANES, axis=1)
                lg = p0 + (left(p0, 1) + p1) + (left(p0, 2) + left(p1, 1) + p2)
                lg_ref[rs, :] = lg + br_ref[...]

    row = functools.partial(_mod_row, tm=tm, tp=tp, ns=ns)
    full = lambda shape: pl.BlockSpec(shape, lambda i: (0,) * len(shape))
    tile = lambda n: pl.BlockSpec((tm, n), lambda i: (i, 0))
    ctx_tile = pl.BlockSpec((tm, GROUP_WIDTH), lambda i: (jnp.minimum(i, np_tiles - 1), 0))
    lat_tile = pl.BlockSpec((tm, GROUP_WIDTH), lambda i: (jnp.maximum(i - np_tiles, 0), 0))
    w_spec = pl.BlockSpec((None,) + w_all.shape[1:], lambda i: (layer, 0, 0),
                          pipeline_mode=pl.Buffered(1))
    in_specs = [ctx_tile] * 3 + [lat_tile] * 3 + [
        tile(GROUP_WIDTH), w_spec, tile(d),
        pl.BlockSpec((1, 6, d), lambda i: (row(i), 0, 0)), full(g1.shape), full(b1.shape)]
    args = [*abc_p, *abc_s, od, w_all, x, mods_l, g1, b1]
    out_shape = [jax.ShapeDtypeStruct((t, d), F32),
                 jax.ShapeDtypeStruct((t, d), F32 if with_router else BF16)]
    out_specs = [tile(d), tile(d)]
    if with_router:
        in_specs += [full(r.shape) for r in router]
        args += list(router)
        out_shape.append(jax.ShapeDtypeStruct((t, LANES), F32))
        out_specs.append(tile(LANES))
    return pl.pallas_call(
        body, out_shape=tuple(out_shape), grid=(t // tm,),
        in_specs=in_specs, out_specs=tuple(out_specs),
        scratch_shapes=[pltpu.VMEM((tm, 4 * GROUP_WIDTH), BF16)],
        compiler_params=_cparams("parallel"),
        name="out_proj",
    )(*args)


def _ffn_up(h2, wg, wu, layer, moe_w, moe_layer, tm, tn):
    t, d = h2.shape
    dff = wg.shape[2]
    nj, ni = dff // tn, t // tm
    _, n_exp, _, dffe = moe_w[0].shape
    rows, cols = d // ni, dffe // nj
    assert rows * ni == d and cols * nj == dffe and rows % 16 == 0 and cols % LANES == 0

    def body(h_ref, wg_ref, wu_ref, mg_ref, mu_ref, md_ref, a_ref, og_ref, ou_ref, od_ref):
        h = h_ref[...]
        a_ref[...] = (_silu(_dot(h, wg_ref[...])) * _dot(h, wu_ref[...])).astype(BF16)
        for src, dst in ((mg_ref, og_ref), (mu_ref, ou_ref), (md_ref, od_ref)):
            dst[...] = src[...].astype(BF16)

    up_in = pl.BlockSpec((None, n_exp, rows, cols), lambda j, i: (moe_layer, 0, i, j))
    up_out = pl.BlockSpec((n_exp, rows, cols), lambda j, i: (0, i, j))
    down_in = pl.BlockSpec((None, n_exp, cols, rows), lambda j, i: (moe_layer, 0, j, i))
    down_out = pl.BlockSpec((n_exp, cols, rows), lambda j, i: (0, j, i))
    return pl.pallas_call(
        body,
        out_shape=(jax.ShapeDtypeStruct((t, dff), BF16),
                   jax.ShapeDtypeStruct((n_exp, d, dffe), BF16),
                   jax.ShapeDtypeStruct((n_exp, d, dffe), BF16),
                   jax.ShapeDtypeStruct((n_exp, dffe, d), BF16)),
        grid=(nj, ni),
        in_specs=[pl.BlockSpec((tm, d), lambda j, i: (i, 0)),
                  pl.BlockSpec((None, d, tn), lambda j, i: (layer, 0, j)),
                  pl.BlockSpec((None, d, tn), lambda j, i: (layer, 0, j)),
                  up_in, up_in, down_in],
        out_specs=(pl.BlockSpec((tm, tn), lambda j, i: (i, j)), up_out, up_out, down_out),
        compiler_params=_cparams("parallel", "parallel"),
        name="ffn_up",
    )(h2, wg, wu, *moe_w)


def _ffn_down(a, wd, layer, x1, mods_l, mods_next, g2, b2, tp, ns, tm):
    t, d = x1.shape
    dff = a.shape[1]

    def body(a_ref, wd_ref, x_ref, m_ref, mn_ref, g_ref, b_ref, o_ref, hn_ref):
        f = _dot(a_ref[...], wd_ref[...])
        x2 = _post_ln(x_ref[...], f, m_ref[0, 5:6, :], g_ref[...], b_ref[...])
        o_ref[...] = x2
        hn_ref[...] = _mod_ln(x2, mn_ref, 0).astype(BF16)

    row = functools.partial(_mod_row, tm=tm, tp=tp, ns=ns)
    full = lambda shape: pl.BlockSpec(shape, lambda i: (0,) * len(shape))
    mod_spec = pl.BlockSpec((1, 6, d), lambda i: (row(i), 0, 0))
    tile = pl.BlockSpec((tm, d), lambda i: (i, 0))
    return pl.pallas_call(
        body,
        out_shape=(jax.ShapeDtypeStruct((t, d), F32), jax.ShapeDtypeStruct((t, d), BF16)),
        grid=(t // tm,),
        in_specs=[pl.BlockSpec((tm, dff), lambda i: (i, 0)),
                  pl.BlockSpec((None, dff, d), lambda i: (layer, 0, 0), pipeline_mode=pl.Buffered(1)),
                  tile, mod_spec, mod_spec, full(g2.shape), full(b2.shape)],
        out_specs=(tile, tile),
        compiler_params=_cparams("parallel"),
        name="ffn_down",
    )(a, wd, x1, mods_l, mods_next, g2, b2)


ROUTE_I1, ROUTE_I2, ROUTE_W1, ROUTE_W2, ROUTE_R1, ROUTE_R2 = range(6)


def _route(logits, tr):
    t = logits.shape[0]

    def body(lg_ref, route_ref, cnt_ref, carry_ref):
        @pl.when(pl.program_id(0) == 0)
        def _():
            carry_ref[...] = jnp.zeros_like(carry_ref)

        lane = lax.broadcasted_iota(jnp.int32, (tr, LANES), 1)
        lane_f = lane.astype(F32)
        lg = jnp.where(lane < N_EXPERTS, lg_ref[...], -jnp.inf)
        m1 = jnp.max(lg, axis=-1, keepdims=True)
        i1 = jnp.min(jnp.where(lg == m1, lane_f, float(LANES)), axis=-1, keepdims=True)
        lg2 = jnp.where(lane_f == i1, -jnp.inf, lg)
        m2 = jnp.max(lg2, axis=-1, keepdims=True)
        i2 = jnp.min(jnp.where(lg2 == m2, lane_f, float(LANES)), axis=-1, keepdims=True)
        e2 = jnp.exp(m2 - m1)
        den = 1.0 + e2
        oh1 = lane_f == i1
        oh2 = lane_f == i2
        sel = jnp.where(oh1 | oh2, 1.0, 0.0)
        ri = lax.broadcasted_iota(jnp.int32, (tr, tr), 0)
        ci = lax.broadcasted_iota(jnp.int32, (tr, tr), 1)
        tri = jnp.where(ri > ci, 1.0, 0.0).astype(BF16)
        prefix = _dot(tri, sel.astype(BF16)) + carry_ref[0:1, :]
        r1 = jnp.sum(jnp.where(oh1, prefix, 0.0), axis=-1, keepdims=True)
        r2 = jnp.sum(jnp.where(oh2, prefix, 0.0), axis=-1, keepdims=True)
        carry_ref[0:1, :] = carry_ref[0:1, :] + jnp.sum(sel, axis=0, keepdims=True)
        out = jnp.zeros((tr, LANES), F32)
        for idx, val in ((ROUTE_I1, i1), (ROUTE_I2, i2), (ROUTE_W1, 1.0 / den),
                         (ROUTE_W2, e2 / den), (ROUTE_R1, r1), (ROUTE_R2, r2)):
            out = jnp.where(lane == idx, val, out)
        route_ref[...] = out
        cnt_ref[...] = carry_ref[...]

    return pl.pallas_call(
        body,
        out_shape=(jax.ShapeDtypeStruct((t, LANES), F32), jax.ShapeDtypeStruct((8, LANES), F32)),
        grid=(t // tr,),
        in_specs=[pl.BlockSpec((tr, LANES), lambda i: (i, 0))],
        out_specs=(pl.BlockSpec((tr, LANES), lambda i: (i, 0)),
                   pl.BlockSpec((8, LANES), lambda i: (0, 0))),
        scratch_shapes=[pltpu.VMEM((8, LANES), F32)],
        compiler_params=_cparams("arbitrary"),
        name="route",
    )(logits)


ROW_COPY_UNROLL = 8


def _row_sources(pos1, pos2, n_rows):
    t = pos1.shape[0]

    def body(p1_ref, p2_ref, src_ref):
        def clear(r, carry):
            src_ref[r] = 0
            return carry

        def put(tok, carry):
            src_ref[p1_ref[tok]] = tok
            src_ref[p2_ref[tok]] = tok
            return carry

        lax.fori_loop(0, n_rows, clear, 0, unroll=ROW_COPY_UNROLL)
        lax.fori_loop(0, t, put, 0, unroll=ROW_COPY_UNROLL)

    smem = pl.BlockSpec(memory_space=pltpu.SMEM)
    return pl.pallas_call(
        body,
        out_shape=jax.ShapeDtypeStruct((n_rows,), jnp.int32),
        in_specs=[smem, smem], out_specs=smem,
        name="row_sources",
    )(pos1, pos2)


def _experts(tile_expert, n_used, src, h, wg, wu, wd, tm, ck):
    d = h.shape[1]
    nt = src.shape[0] // tm
    dff = wg.shape[2]

    def row_copy(src_ref, h_ref, xg_ref, sem, tile, slot, r):
        return pltpu.make_async_copy(h_ref.at[pl.ds(src_ref[tile * tm + r], 1)],
                                     xg_ref.at[slot, pl.ds(r, 1)], sem.at[slot])

    def body(te_ref, nu_ref, src_ref, h_ref, wg_ref, wu_ref, wd_ref, y_ref, xg_ref, a_ref, sem):
        del te_ref
        i = pl.program_id(0)
        nu = nu_ref[0]

        def issue(tile, slot):
            def step(r, carry):
                row_copy(src_ref, h_ref, xg_ref, sem, tile, slot, r).start()
                return carry
            lax.fori_loop(0, tm, step, 0, unroll=ROW_COPY_UNROLL)

        def drain(tile, slot):
            del tile
            pltpu.make_async_copy(h_ref.at[pl.ds(0, tm)], xg_ref.at[slot], sem.at[slot]).wait()

        @pl.when(i == 0)
        def _():
            issue(0, 0)

        @pl.when(i < nu)
        def _():
            slot = i & 1
            drain(i, slot)
            x = xg_ref[slot].astype(BF16)
            nxt = jnp.minimum(i + 1, nu - 1)
            nc = dff // ck
            for c in range(nc):
                cs = slice(c * ck, (c + 1) * ck)
                a_ref[:, cs] = (_silu(_dot(x, wg_ref[:, cs])) * _dot(x, wu_ref[:, cs])).astype(BF16)
                for r in range(c * tm // nc, (c + 1) * tm // nc):
                    row_copy(src_ref, h_ref, xg_ref, sem, nxt, 1 - slot, r).start()
            y_ref[...] = _dot(a_ref[...], wd_ref[...])

            @pl.when(i == nu - 1)
            def _():
                drain(nxt, 1 - slot)

        @pl.when(i >= nu)
        def _():
            y_ref[...] = jnp.zeros_like(y_ref)

    def expert_weights(shape):
        return pl.BlockSpec((None,) + shape, lambda i, te, nu, src: (te[i], 0, 0),
                            pipeline_mode=pl.Buffered(1))

    return pl.pallas_call(
        body,
        out_shape=jax.ShapeDtypeStruct((nt * tm, d), F32),
        grid_spec=pltpu.PrefetchScalarGridSpec(
            num_scalar_prefetch=3, grid=(nt,),
            in_specs=[pl.BlockSpec(memory_space=pl.ANY),
                      expert_weights((d, dff)), expert_weights((d, dff)),
                      expert_weights((dff, d))],
            out_specs=pl.BlockSpec((tm, d), lambda i, te, nu, src: (i, 0)),
            scratch_shapes=[pltpu.VMEM((2, tm, d), F32), pltpu.VMEM((tm, dff), BF16),
                            pltpu.SemaphoreType.DMA((2,))]),
        compiler_params=_cparams("arbitrary"),
        name="experts",
    )(tile_expert, n_used, src, h, wg, wu, wd)


COMBINE_ROWS = 32


def _combine(pos1, pos2, ys, route, x1, mods_l, mods_next, g2, b2, tp, ns, tc):
    t, d = x1.shape
    is_last = mods_next is None
    np_tiles = tp // tc

    def row_copies(p1_ref, p2_ref, ys_ref, y1_ref, y2_ref, sem, tile, slot, r):
        tok = tile * tc + r
        return (pltpu.make_async_copy(ys_ref.at[pl.ds(p1_ref[tok], 1)],
                                      y1_ref.at[slot, pl.ds(r, 1)], sem.at[0, slot]),
                pltpu.make_async_copy(ys_ref.at[pl.ds(p2_ref[tok], 1)],
                                      y2_ref.at[slot, pl.ds(r, 1)], sem.at[1, slot]))

    def body(p1_ref, p2_ref, ys_ref, rt_ref, x_ref, m_ref, *rest):
        if is_last:
            g_ref, b_ref, yp_ref, yl_ref, y1_ref, y2_ref, sem, o_ref = rest
        else:
            mn_ref, g_ref, b_ref, o_ref, hn_ref, y1_ref, y2_ref, sem = rest
        j = pl.program_id(0)

        def issue(tile, slot):
            def step(r, carry):
                for cp in row_copies(p1_ref, p2_ref, ys_ref, y1_ref, y2_ref, sem, tile, slot, r):
                    cp.start()
                return carry
            lax.fori_loop(0, tc, step, 0, unroll=ROW_COPY_UNROLL)

        def drain(slot):
            for k, buf in enumerate((y1_ref, y2_ref)):
                pltpu.make_async_copy(ys_ref.at[pl.ds(0, tc)], buf.at[slot], sem.at[k, slot]).wait()

        @pl.when(j == 0)
        def _():
            issue(0, 0)

        slot = j & 1
        drain(slot)
        last = pl.num_programs(0) - 1
        nxt = jnp.minimum(j + 1, last)
        for blk in range(tc // COMBINE_ROWS):
            rs = slice(blk * COMBINE_ROWS, (blk + 1) * COMBINE_ROWS)
            rt = rt_ref[rs, :]
            f = (rt[:, ROUTE_W1:ROUTE_W1 + 1] * y1_ref[slot, rs, :]
                 + rt[:, ROUTE_W2:ROUTE_W2 + 1] * y2_ref[slot, rs, :])
            x2 = _post_ln(x_ref[rs, :], f, m_ref[0, 5:6, :], g_ref[...], b_ref[...])
            o_ref[rs, :] = x2
            if not is_last:
                hn_ref[rs, :] = _mod_ln(x2, mn_ref, 0).astype(BF16)
            for r in range(blk * COMBINE_ROWS, (blk + 1) * COMBINE_ROWS):
                for cp in row_copies(p1_ref, p2_ref, ys_ref, y1_ref, y2_ref, sem, nxt, 1 - slot, r):
                    cp.start()

        @pl.when(j == last)
        def _():
            drain(1 - slot)

        if is_last:
            @pl.when(j < np_tiles)
            def _():
                yp_ref[...] = o_ref[...]

            @pl.when(j >= np_tiles)
            def _():
                yl_ref[...] = o_ref[...]

    row = functools.partial(_mod_row, tm=tc, tp=tp, ns=ns)
    full = lambda shape: pl.BlockSpec(shape, lambda i, p1, p2: (0,) * len(shape))
    tile = pl.BlockSpec((tc, d), lambda i, p1, p2: (i, 0))
    mod_spec = pl.BlockSpec((1, 6, d), lambda i, p1, p2: (row(i), 0, 0))
    in_specs = [pl.BlockSpec(memory_space=pl.ANY),
                pl.BlockSpec((tc, LANES), lambda i, p1, p2: (i, 0)), tile, mod_spec]
    args = [pos1, pos2, ys, route, x1, mods_l]
    scratch = [pltpu.VMEM((2, tc, d), F32), pltpu.VMEM((2, tc, d), F32),
               pltpu.SemaphoreType.DMA((2, 2))]
    if is_last:
        out_shape = (jax.ShapeDtypeStruct((tp, d), F32), jax.ShapeDtypeStruct((t - tp, d), F32))
        out_specs = (pl.BlockSpec((tc, d), lambda i, p1, p2: (jnp.minimum(i, np_tiles - 1), 0)),
                     pl.BlockSpec((tc, d), lambda i, p1, p2: (jnp.maximum(i - np_tiles, 0), 0)))
        scratch.append(pltpu.VMEM((tc, d), F32))
    else:
        in_specs.append(mod_spec)
        args.append(mods_next)
        out_shape = (jax.ShapeDtypeStruct((t, d), F32), jax.ShapeDtypeStruct((t, d), BF16))
        out_specs = (tile, tile)
    in_specs += [full(g2.shape), full(b2.shape)]
    args += [g2, b2]
    return pl.pallas_call(
        body,
        out_shape=out_shape,
        grid_spec=pltpu.PrefetchScalarGridSpec(
            num_scalar_prefetch=2, grid=(t // tc,),
            in_specs=in_specs, out_specs=out_specs, scratch_shapes=scratch),
        compiler_params=_cparams("arbitrary"),
        name="combine",
    )(*args)


def _ffn_routed(h2, logits, wg, wu, wd, x1, mods_l, mods_next, g2, b2, tp, ns, tm, ck):
    t = h2.shape[0]
    route, cnt = _route(logits, min(512, t))
    i1 = route[:, ROUTE_I1].astype(jnp.int32)
    i2 = route[:, ROUTE_I2].astype(jnp.int32)
    counts = cnt[0, :N_EXPERTS].astype(jnp.int32)
    padded = (counts + tm - 1) // tm * tm
    ends = jnp.cumsum(padded)
    starts = ends - padded
    pos1 = starts[i1] + route[:, ROUTE_R1].astype(jnp.int32)
    pos2 = starts[i2] + route[:, ROUTE_R2].astype(jnp.int32)
    nt = 2 * t // tm + N_EXPERTS
    n_used = ends[-1] // tm
    tile_start = jnp.arange(nt, dtype=jnp.int32) * tm
    tile_expert = jnp.sum(tile_start[:, None] >= ends[None, :], axis=1).astype(jnp.int32)
    tile_expert = jnp.minimum(tile_expert, tile_expert[n_used - 1])
    src = _row_sources(pos1, pos2, nt * tm)
    ys = _experts(tile_expert, n_used.reshape(1), src, h2, wg, wu, wd, tm, ck)
    return _combine(pos1, pos2, ys, route, x1, mods_l, mods_next, g2, b2, tp, ns, tm)


def _permute_in_cols(w):
    d = w.shape[1]
    qk = w[..., :4 * A_HEADS * A_QK].reshape(w.shape[0], d, 2, 2, A_HEADS, A_QK)
    qk = qk.transpose(0, 1, 2, 4, 3, 5).reshape(w.shape[0], d, 4 * A_HEADS * A_QK)
    return jnp.concatenate([qk, w[..., 4 * A_HEADS * A_QK:]], axis=-1)


def kernel(x_prompt, x_sample, cache_a_k, cache_a_v, cache_c_k, cache_c_v, c, c_ctx, w_ada, b_ada, w_in, lam, a_subln, w_fnet, c_sink, d_norm_g, d_norm_b, d_ws, d_bs, w_out, ln1_g, ln1_b, ln2_g, ln2_b, ffn_wg, ffn_wu, ffn_wd, w_router, b_router, moe_wg, moe_wu, moe_wd):
    nb_p, n_p, d = x_prompt.shape
    nb_s, n_s, _ = x_sample.shape
    depth = w_in.shape[0]
    past = cache_a_k.shape[2]
    tp = nb_p * n_p
    t = tp + nb_s * n_s
    assert tp % n_s == 0 and n_s % 512 == 0 and n_p % CHUNK == 0 and nb_s + 1 <= 8

    assert depth % 2 == 0
    c_all = jnp.zeros((8, d), F32).at[0].set(c_ctx).at[1:1 + nb_s].set(c)
    mods = _modulation(c_all, w_ada, b_ada).reshape(depth, 8, 6, d)

    w_in_b = _permute_in_cols(w_in.astype(BF16))
    w_out_b = w_out.astype(BF16)
    w_fnet_b = w_fnet.astype(BF16)
    d_ws_b = d_ws.astype(BF16)
    ffn_wg_b, ffn_wu_b, ffn_wd_b = (w.astype(BF16) for w in (ffn_wg, ffn_wu, ffn_wd))
    rope_a = _rope_tables(n_s, A_QK)
    rope_c = _rope_tables(n_s, C_HD)
    csh, csl = _split(jnp.asarray(_dft_channel_table()))
    w2p_h = jnp.asarray(_dft_table(n_p)).astype(BF16)
    w2s_h = jnp.asarray(_dft_table(n_s)).astype(BF16)
    ca_k = cache_a_k.reshape(nb_s, depth, past, A_HEADS * A_V)
    ca_v = cache_a_v.reshape(nb_s, depth, past, A_HEADS * A_V)
    cc_k = cache_c_k.reshape(nb_s, depth, past, C_KV_HEADS * C_HD)
    cc_v = cache_c_v.reshape(nb_s, depth, past, C_KV_HEADS * C_HD)

    tm_big = math.gcd(1024, tp, n_s)
    tm_mid = math.gcd(512, tp, n_s)
    tm_small = math.gcd(256, tp, n_s)
    new_ak, new_av, new_ck, new_cv = [], [], [], []
    x, h = _first_ln(x_prompt.reshape(tp, d), x_sample.reshape(nb_s * n_s, d), mods[0], n_s, tm_mid)
    for l in range(depth):
        lam_init = 0.8 - 0.6 * math.exp(-0.3 * l)
        mods_next = mods[l + 1] if l + 1 < depth else None
        proj, ak, av, ck, cv = _in_proj(h, w_in_b, l, tp, tm_big)
        new_ak.append(ak.reshape(nb_p, n_p, A_HEADS, 2 * A_QK))
        new_av.append(av.reshape(nb_p, n_p, A_HEADS, A_V))
        new_ck.append(ck.reshape(nb_p, n_p, C_KV_HEADS, C_HD))
        new_cv.append(cv.reshape(nb_p, n_p, C_KV_HEADS, C_HD))

        subln = a_subln[l].reshape(1, A_V)
        abc_p = _prompt_mixers(proj, lam[l], subln, c_sink[l], w2p_h, csh, csl,
                               w_fnet_b[l], lam_init, nb_p, n_p)
        oa_s = _sample_diff_attn(proj, ca_k, ca_v, l, rope_a[0], rope_a[1], lam[l], subln,
                                 lam_init, tp, nb_s, n_s, 256)
        oc_s = _sample_window_attn(proj, cc_k, cc_v, l, rope_c[0], rope_c[1], c_sink[l],
                                   tp, nb_s, n_s)
        ob_s = _sample_fourier(proj, w2s_h, csh, csl, w_fnet_b[l], tp, nb_s, n_s, tm_mid)
        abc_s = (oa_s, ob_s, oc_s)
        od = _spatial_gate(proj, d_norm_g[l].reshape(1, -1), d_norm_b[l].reshape(1, -1),
                           d_ws_b[l], d_bs[l].T, 256)

        g1, b1 = ln1_g[l].reshape(1, d), ln1_b[l].reshape(1, d)
        g2, b2 = ln2_g[l].reshape(1, d), ln2_b[l].reshape(1, d)
        j = l // 2
        if l % 2 == 0:
            x1, h2 = _out_proj(abc_p, abc_s, od, w_out_b, l, x, mods[l], g1, b1, None,
                               tp, n_s, tm_mid)
            a, *moe_b = _ffn_up(h2, ffn_wg_b, ffn_wu_b, j, (moe_wg, moe_wu, moe_wd), j, tm_big, 512)
            x, h = _ffn_down(a, ffn_wd_b, j, x1, mods[l], mods_next, g2, b2, tp, n_s, tm_small)
        else:
            router = _pack_router(w_router[j], b_router[j])
            x1, h2, logits = _out_proj(abc_p, abc_s, od, w_out_b, l, x, mods[l], g1, b1, router,
                                       tp, n_s, tm_mid)
            x, h = _ffn_routed(h2, logits, *moe_b, x1, mods[l], mods_next,
                               g2, b2, tp, n_s, tm_small, 256)

    y_p = x.reshape(nb_p, n_p, d)
    y_s = h.reshape(nb_s, n_s, d)
    return (y_p, y_s, jnp.stack(new_ak, axis=1), jnp.stack(new_av, axis=1),
            jnp.stack(new_ck, axis=1), jnp.stack(new_cv, axis=1))
```

```python
import functools
import math

import jax
import jax.numpy as jnp
import numpy as np
from jax import lax
from jax.experimental import pallas as pl
from jax.experimental.pallas import tpu as pltpu

F32 = jnp.float32
BF16 = jnp.bfloat16

GRID_W = 64
A_V = 128
A_QK = 64
A_HEADS = 4
B_GD = 128
B_GROUPS = 4
C_HD = 128
C_HEADS = 4
C_KV_HEADS = 2
C_GROUP = C_HEADS // C_KV_HEADS
C_WINDOW = 128
D_HD = 128
D_HEADS = 4
CHUNK = 128
GROUP_WIDTH = 512
N_EXPERTS = 8
ROPE_BASE = 10000.0
LN_EPS = 1e-6
SUBLN_EPS = 1e-5
NEG = -1e30
TOTAL_DEPTH = 4
ALPHA = (2.0 * TOTAL_DEPTH) ** 0.25

COL_AQ = 0
COL_AK = 512
COL_AV = 1024
COL_BU = 1536
COL_CQ = 2048
COL_CK = 2560
COL_CV = 2816
COL_DU = 3072
COL_DV = 3584

LANES = 128
V7X_VMEM_BYTES = 64 * 1024 * 1024
VMEM_LIMIT = V7X_VMEM_BYTES - 8 * 1024 * 1024


def _cparams(*sem):
    return pltpu.CompilerParams(dimension_semantics=sem, vmem_limit_bytes=VMEM_LIMIT)


def _dot(a, b):
    return jnp.dot(a, b, preferred_element_type=F32)


def _dot_nt(a, b):
    return lax.dot_general(a, b, (((1,), (1,)), ((), ())), preferred_element_type=F32)


def _split(a):
    hi = a.astype(BF16)
    lo = (a - hi.astype(F32)).astype(BF16)
    return hi, lo


def _split3(a):
    a0 = a.astype(BF16)
    r1 = a - a0.astype(F32)
    a1 = r1.astype(BF16)
    a2 = (r1 - a1.astype(F32)).astype(BF16)
    return a0, a1, a2


def _dot3(a_hi, a_lo, b_hi, b_lo):
    return _dot(a_hi, b_hi) + (_dot(a_hi, b_lo) + _dot(a_lo, b_hi))


def _ln_rows(x):
    xc = x - jnp.mean(x, axis=-1, keepdims=True)
    return xc * lax.rsqrt(jnp.mean(xc * xc, axis=-1, keepdims=True) + LN_EPS)


def _silu(x):
    return x / (1.0 + jnp.exp(-x))


def _mod_row(i, tm, tp, ns):
    return jnp.where(i * tm < tp, 0, 1 + (i * tm - tp) // ns)


def _modulation(c_all, w_ada, b_ada):
    depth, d, n6 = w_ada.shape
    rows = c_all.shape[0]
    tn = 512

    def body(c_ref, w_ref, b_ref, o_ref):
        s = _silu(c_ref[...]).astype(BF16)
        o_ref[0] = _dot(s, w_ref[0].astype(BF16)) + b_ref[0]

    return pl.pallas_call(
        body,
        out_shape=jax.ShapeDtypeStruct((depth, rows, n6), F32),
        grid=(depth, n6 // tn),
        in_specs=[pl.BlockSpec((rows, d), lambda l, j: (0, 0)),
                  pl.BlockSpec((1, d, tn), lambda l, j: (l, 0, j)),
                  pl.BlockSpec((1, 1, tn), lambda l, j: (l, 0, j))],
        out_specs=pl.BlockSpec((1, rows, tn), lambda l, j: (l, 0, j)),
        compiler_params=_cparams("parallel", "parallel"),
        name="modulation",
    )(c_all, w_ada, b_ada.reshape(depth, 1, n6))


def _mod_ln(x, m_ref, k):
    return _ln_rows(x) * (1.0 + m_ref[0, k + 1:k + 2, :]) + m_ref[0, k:k + 1, :]


def _first_ln(x_p, x_s, mods_l, ns, tm):
    tp, d = x_p.shape
    t = tp + x_s.shape[0]
    np_tiles = tp // tm

    def body(xp_ref, xs_ref, m_ref, x_ref, h_ref):
        x = jnp.where(pl.program_id(0) < np_tiles, xp_ref[...], xs_ref[...])
        x_ref[...] = x
        h_ref[...] = _mod_ln(x, m_ref, 0).astype(BF16)

    row = functools.partial(_mod_row, tm=tm, tp=tp, ns=ns)
    tile = pl.BlockSpec((tm, d), lambda i: (i, 0))
    return pl.pallas_call(
        body,
        out_shape=(jax.ShapeDtypeStruct((t, d), F32), jax.ShapeDtypeStruct((t, d), BF16)),
        grid=(t // tm,),
        in_specs=[pl.BlockSpec((tm, d), lambda i: (jnp.minimum(i, np_tiles - 1), 0)),
                  pl.BlockSpec((tm, d), lambda i: (jnp.maximum(i - np_tiles, 0), 0)),
                  pl.BlockSpec((1, 6, d), lambda i: (row(i), 0, 0))],
        out_specs=(tile, tile),
        compiler_params=_cparams("parallel"),
        name="first_ln",
    )(x_p, x_s, mods_l)


IN_TN = 1024
CACHE_CUTS = ((COL_AK // IN_TN, COL_AK % IN_TN, COL_AV - COL_AK),
              (COL_AV // IN_TN, COL_AV % IN_TN, COL_BU - COL_AV),
              (COL_CK // IN_TN, COL_CK % IN_TN, COL_CV - COL_CK),
              (COL_CV // IN_TN, COL_CV % IN_TN, COL_DU - COL_CV))


def _in_proj(h, w_all, layer, caches, tp, tm):
    t, d = h.shape
    n = w_all.shape[2]
    tn = IN_TN
    np_tiles = tp // tm
    n_p = caches[0].shape[2]
    bpt = tm // n_p
    assert bpt * n_p == tm

    def body(h_ref, w_ref, *rest):
        o_ref = rest[len(caches)]
        cache_refs = rest[len(caches) + 1:]
        i = pl.program_id(0)
        j = pl.program_id(1)
        o_ref[...] = _dot(h_ref[...], w_ref[...])

        for c_ref, (jt, c0, width) in zip(cache_refs, CACHE_CUTS):
            @pl.when(jnp.logical_and(j == jt, i < np_tiles))
            def _(c_ref=c_ref, c0=c0, width=width):
                for bb in range(bpt):
                    c_ref[bb] = o_ref[bb * n_p:(bb + 1) * n_p, c0:c0 + width]

    def cache_spec(c):
        return pl.BlockSpec((bpt, None, n_p, c.shape[3]),
                            lambda i, j: (jnp.minimum(i, np_tiles - 1), layer, 0, 0))

    n_in = 2
    return pl.pallas_call(
        body,
        out_shape=(jax.ShapeDtypeStruct((t, n), F32),
                   *[jax.ShapeDtypeStruct(c.shape, c.dtype) for c in caches]),
        grid=(t // tm, n // tn),
        in_specs=[pl.BlockSpec((tm, d), lambda i, j: (i, 0)),
                  pl.BlockSpec((None, d, tn), lambda i, j: (layer, 0, j)),
                  *[pl.BlockSpec(memory_space=pl.ANY) for _ in caches]],
        out_specs=(pl.BlockSpec((tm, tn), lambda i, j: (i, j)), *[cache_spec(c) for c in caches]),
        input_output_aliases={n_in + k: 1 + k for k in range(len(caches))},
        compiler_params=_cparams("parallel", "arbitrary"),
        name="in_proj",
    )(h, w_all, *caches)


def _lam_value(lam_ref, lam_init):
    lv = lam_ref[...]
    d1 = jnp.sum(lv[0:1] * lv[1:2], axis=-1, keepdims=True)
    d2 = jnp.sum(lv[2:3] * lv[3:4], axis=-1, keepdims=True)
    return jnp.exp(d1) - jnp.exp(d2) + lam_init


def _softmax(s):
    e = jnp.exp(s - jnp.max(s, axis=-1, keepdims=True))
    return e / jnp.sum(e, axis=-1, keepdims=True)


def _diff_attend(q, kb, vb, lam):
    lane = lax.broadcasted_iota(jnp.int32, q.shape, 1)
    q1 = jnp.where(lane < A_QK, q, 0.0).astype(BF16)
    q2 = jnp.where(lane >= A_QK, q, 0.0).astype(BF16)
    scale = A_QK ** -0.5
    s1 = _dot_nt(q1, kb) * scale
    s2 = _dot_nt(q2, kb) * scale
    p = _softmax(s1) - lam * _softmax(s2)
    return _dot(p.astype(BF16), vb)


KEY_CHUNK = 256


def _diff_attend_chunked(q, kr_ref, vb_ref, lam, kc):
    lane = lax.broadcasted_iota(jnp.int32, q.shape, 1)
    qs = q * (A_QK ** -0.5)
    q1 = jnp.where(lane < A_QK, qs, 0.0).astype(BF16)
    q2 = jnp.where(lane >= A_QK, qs, 0.0).astype(BF16)
    chunks = [slice(c * kc, (c + 1) * kc) for c in range(kr_ref.shape[0] // kc)]

    def fold(x, op):
        out = x[:, :LANES]
        for j in range(1, kc // LANES):
            out = op(out, x[:, j * LANES:(j + 1) * LANES])
        return out

    m1 = m2 = None
    for cs in chunks:
        kb = kr_ref[cs, :]
        c1 = fold(_dot_nt(q1, kb), jnp.maximum)
        c2 = fold(_dot_nt(q2, kb), jnp.maximum)
        m1 = c1 if m1 is None else jnp.maximum(m1, c1)
        m2 = c2 if m2 is None else jnp.maximum(m2, c2)
    m1 = jnp.max(m1, axis=-1, keepdims=True)
    m2 = jnp.max(m2, axis=-1, keepdims=True)
    l1 = l2 = o1 = o2 = 0.0
    for cs in chunks:
        kb = kr_ref[cs, :]
        vb = vb_ref[cs, :]
        e1 = jnp.exp(_dot_nt(q1, kb) - m1)
        e2 = jnp.exp(_dot_nt(q2, kb) - m2)
        l1 = l1 + fold(e1, jnp.add)
        l2 = l2 + fold(e2, jnp.add)
        o1 = o1 + _dot(e1.astype(BF16), vb)
        o2 = o2 + _dot(e2.astype(BF16), vb)
    l1 = jnp.sum(l1, axis=-1, keepdims=True)
    l2 = jnp.sum(l2, axis=-1, keepdims=True)
    return o1 / l1 - lam * (o2 / l2)


def _sub_ln(o, g, lam_init):
    r = lax.rsqrt(jnp.mean(o * o, axis=-1, keepdims=True) + SUBLN_EPS)
    return (o * r) * g * (1.0 - lam_init)


def _rope(x, cos, sin_signed, half):
    if 2 * half == LANES:
        swapped = pltpu.roll(x, half, axis=1)
    else:
        lane = lax.broadcasted_iota(jnp.int32, x.shape, 1)
        first = (lane % (2 * half)) < half
        swapped = jnp.where(first, pltpu.roll(x, LANES - half, axis=1),
                            pltpu.roll(x, half, axis=1))
    return x * cos + swapped * sin_signed


def _rope_tables(n_tokens, head_dim):
    rows = n_tokens // GRID_W
    row = np.repeat(np.arange(rows, dtype=np.float64), GRID_W)
    col = np.tile(np.arange(GRID_W, dtype=np.float64), rows)
    n_freq = head_dim // 4
    inv = np.power(ROPE_BASE, -np.arange(n_freq, dtype=np.float64) / n_freq)
    ang = np.concatenate([row[:, None] * inv, col[:, None] * inv], axis=-1)
    cos, sin = np.cos(ang), np.sin(ang)
    reps = LANES // head_dim
    cos_t = np.tile(np.concatenate([cos, cos], axis=-1), (1, reps))
    sin_t = np.tile(np.concatenate([-sin, sin], axis=-1), (1, reps))
    return cos_t.astype(np.float32), sin_t.astype(np.float32)


def _dft_cos_sin(n):
    jk = np.outer(np.arange(n), np.arange(n)) % n
    ang = jk.astype(np.float64) * (2.0 * math.pi / n)
    return np.cos(ang) * n ** -0.5, np.sin(ang) * n ** -0.5


def _dft_table(n):
    c, s = _dft_cos_sin(n)
    return np.concatenate([c, -s], axis=1).astype(np.float32)


def _dft_channel_table():
    c, s = _dft_cos_sin(B_GD)
    return np.concatenate([c, s], axis=1).astype(np.float32)


def _fourier_stage1(u, csh, csl):
    uh, ul = _split(u)
    t = _dot3(uh, ul, csh, csl)
    return jnp.concatenate([t[:, :B_GD], t[:, B_GD:]], axis=0)


PROMPT_BATCHES_PER_STEP = 2


def _prompt_mixers(proj, lam_l, subln_l, sink_l, w2, csh, csl, wf, lam_init, nbatch, n):
    t = nbatch * n
    bps = math.gcd(PROMPT_BATCHES_PER_STEP, nbatch)

    def body(p_ref, lam_ref, g_ref, sink_ref, w2_ref, csh_ref, csl_ref, wf_ref,
             oa_ref, ob_ref, oc_ref):
        lam = _lam_value(lam_ref, lam_init)
        scale = C_HD ** -0.5
        for bb in range(bps):
            rows = slice(bb * n, (bb + 1) * n)
            for h in range(A_HEADS):
                q = p_ref[rows, COL_AQ + h * A_V:COL_AQ + (h + 1) * A_V]
                kb = p_ref[rows, COL_AK + h * A_V:COL_AK + (h + 1) * A_V].astype(BF16)
                vb = p_ref[rows, COL_AV + h * A_V:COL_AV + (h + 1) * A_V].astype(BF16)
                o = _diff_attend(q, kb, vb, lam)
                oa_ref[rows, h * A_V:(h + 1) * A_V] = _sub_ln(o, g_ref[...], lam_init).astype(BF16)

            acc = None
            for g in range(B_GROUPS):
                u = p_ref[rows, COL_BU + g * B_GD:COL_BU + (g + 1) * B_GD]
                tb = _fourier_stage1(u, csh_ref[...], csl_ref[...]).astype(BF16)
                f = _dot(w2_ref[...], tb)
                part = _dot(f.astype(BF16), wf_ref[g * B_GD:(g + 1) * B_GD, :])
                acc = part if acc is None else acc + part
            ob_ref[rows, :] = acc.astype(BF16)

            for hq in range(C_HEADS):
                kvh = hq // C_GROUP
                q = p_ref[rows, COL_CQ + hq * C_HD:COL_CQ + (hq + 1) * C_HD].astype(BF16)
                kb = p_ref[rows, COL_CK + kvh * C_HD:COL_CK + (kvh + 1) * C_HD].astype(BF16)
                vb = p_ref[rows, COL_CV + kvh * C_HD:COL_CV + (kvh + 1) * C_HD].astype(BF16)
                s = _dot_nt(q, kb) * scale
                snk = sink_ref[hq]
                m = jnp.maximum(jnp.max(s, axis=-1, keepdims=True), snk)
                e = jnp.exp(s - m)
                den = jnp.sum(e, axis=-1, keepdims=True) + jnp.exp(snk - m)
                o = _dot((e / den).astype(BF16), vb)
                oc_ref[rows, hq * C_HD:(hq + 1) * C_HD] = o.astype(BF16)

    full = lambda shape: pl.BlockSpec(shape, lambda b: (0,) * len(shape))
    out = jax.ShapeDtypeStruct((t, GROUP_WIDTH), BF16)
    ospec = pl.BlockSpec((bps * n, GROUP_WIDTH), lambda b: (b, 0))
    return pl.pallas_call(
        body,
        out_shape=(out, out, out),
        grid=(nbatch // bps,),
        in_specs=[pl.BlockSpec((bps * n, proj.shape[1]), lambda b: (b, 0)),
                  full(lam_l.shape), full(subln_l.shape),
                  pl.BlockSpec(memory_space=pltpu.SMEM),
                  full(w2.shape), full(csh.shape), full(csl.shape), full(wf.shape)],
        out_specs=(ospec, ospec, ospec),
        compiler_params=_cparams("parallel"),
        name="prompt_mixers",
    )(proj, lam_l, subln_l, sink_l, w2, csh, csl, wf)


def _sample_diff_attn(proj, cache_k, cache_v, layer, cos_t, sin_t, lam_l, subln_l,
                      lam_init, tp, nbatch, ns, tq):
    past = cache_k.shape[2]
    nk = ns + past
    qb0 = tp // tq
    qpb = ns // tq
    kb0 = tp // ns

    def body(q_ref, k_ref, v_ref, ck_ref, cv_ref, cq_ref, sq_ref, cosk_ref, sink_ref,
             lam_ref, g_ref, o_ref, kr_ref, vb_ref):
        @pl.when(pl.program_id(2) == 0)
        def _():
            kr_ref[0:ns, :] = _rope(k_ref[...], cosk_ref[...], sink_ref[...], A_QK // 2).astype(BF16)
            kr_ref[ns:nk, :] = ck_ref[...].astype(BF16)
            vb_ref[0:ns, :] = v_ref[...].astype(BF16)
            vb_ref[ns:nk, :] = cv_ref[...].astype(BF16)

        lam = _lam_value(lam_ref, lam_init)
        q = _rope(q_ref[...], cq_ref[...], sq_ref[...], A_QK // 2)
        o = _diff_attend_chunked(q, kr_ref, vb_ref, lam, KEY_CHUNK)
        o_ref[...] = _sub_ln(o, g_ref[...], lam_init).astype(BF16)

    assert nk % KEY_CHUNK == 0
    cache_spec = pl.BlockSpec((None, None, past, A_V), lambda b, h, i: (b, layer, 0, h))
    full = lambda shape: pl.BlockSpec(shape, lambda b, h, i: (0,) * len(shape))
    return pl.pallas_call(
        body,
        out_shape=jax.ShapeDtypeStruct((nbatch * ns, GROUP_WIDTH), BF16),
        grid=(nbatch, A_HEADS, qpb),
        in_specs=[pl.BlockSpec((tq, A_V), lambda b, h, i: (qb0 + b * qpb + i, COL_AQ // A_V + h)),
                  pl.BlockSpec((ns, A_V), lambda b, h, i: (kb0 + b, COL_AK // A_V + h)),
                  pl.BlockSpec((ns, A_V), lambda b, h, i: (kb0 + b, COL_AV // A_V + h)),
                  cache_spec, cache_spec,
                  pl.BlockSpec((tq, LANES), lambda b, h, i: (i, 0)),
                  pl.BlockSpec((tq, LANES), lambda b, h, i: (i, 0)),
                  full((ns, LANES)), full((ns, LANES)),
                  full(lam_l.shape), full(subln_l.shape)],
        out_specs=pl.BlockSpec((tq, A_V), lambda b, h, i: (b * qpb + i, h)),
        scratch_shapes=[pltpu.VMEM((nk, A_V), BF16), pltpu.VMEM((nk, A_V), BF16)],
        compiler_params=_cparams("parallel", "parallel", "arbitrary"),
        name="sample_diff_attn",
    )(proj, proj, proj, cache_k, cache_v, cos_t, sin_t, cos_t, sin_t, lam_l, subln_l)


def _sample_window_attn(proj, cache_k, cache_v, layer, cos_t, sin_t, sink_l,
                        tp, nbatch, ns):
    w = C_WINDOW
    past = cache_k.shape[2]
    nblk = ns // w
    qb0 = tp // w
    kb0 = tp // ns
    gw = C_GROUP * C_HD
    scale = C_HD ** -0.5

    def body(q_ref, k_ref, v_ref, ck_ref, cv_ref, cq_ref, sq_ref, cosk_ref, sink_ref,
             snk_ref, o_ref, kr_ref, vb_ref):
        kvh = pl.program_id(1)
        i = pl.program_id(2)

        @pl.when(i == 0)
        def _():
            kr_ref[...] = _rope(k_ref[...], cosk_ref[...], sink_ref[...], C_HD // 2).astype(BF16)
            vb_ref[...] = v_ref[...].astype(BF16)

        qs = [_rope(q_ref[:, g * C_HD:(g + 1) * C_HD], cq_ref[...], sq_ref[...], C_HD // 2)
              for g in range(C_GROUP)]
        qq = jnp.concatenate(qs, axis=0).astype(BF16)
        start = pl.multiple_of(jnp.clip((i - 1) * w, 0, ns - 3 * w), w)
        kw = kr_ref[pl.ds(start, 3 * w), :]
        vw = vb_ref[pl.ds(start, 3 * w), :]
        s_loc = _dot_nt(qq, kw) * scale
        rows = lax.broadcasted_iota(jnp.int32, s_loc.shape, 0)
        cols = lax.broadcasted_iota(jnp.int32, s_loc.shape, 1)
        qpos = i * w + (rows & (w - 1))
        kpos = start + cols
        s_loc = jnp.where(jnp.abs(kpos - qpos) <= w, s_loc, NEG)
        s_ctx = _dot_nt(qq, ck_ref[...].astype(BF16)) * scale
        rcol = lax.broadcasted_iota(jnp.int32, (C_GROUP * w, 1), 0)
        snk = jnp.zeros((C_GROUP * w, 1), F32)
        for g in range(C_GROUP):
            snk = jnp.where(rcol >= g * w, snk_ref[kvh * C_GROUP + g], snk)
        m = jnp.maximum(jnp.maximum(jnp.max(s_loc, axis=-1, keepdims=True),
                                    jnp.max(s_ctx, axis=-1, keepdims=True)), snk)
        e_loc = jnp.exp(s_loc - m)
        e_ctx = jnp.exp(s_ctx - m)
        den = (jnp.sum(e_loc, axis=-1, keepdims=True) + jnp.sum(e_ctx, axis=-1, keepdims=True)
               + jnp.exp(snk - m))
        o = (_dot((e_loc / den).astype(BF16), vw)
             + _dot((e_ctx / den).astype(BF16), cv_ref[...].astype(BF16)))
        for g in range(C_GROUP):
            o_ref[:, g * C_HD:(g + 1) * C_HD] = o[g * w:(g + 1) * w, :].astype(BF16)

    cache_spec = pl.BlockSpec((None, None, past, C_HD), lambda b, h, i: (b, layer, 0, h))
    full = lambda shape: pl.BlockSpec(shape, lambda b, h, i: (0,) * len(shape))
    return pl.pallas_call(
        body,
        out_shape=jax.ShapeDtypeStruct((nbatch * ns, GROUP_WIDTH), BF16),
        grid=(nbatch, C_KV_HEADS, nblk),
        in_specs=[pl.BlockSpec((w, gw), lambda b, h, i: (qb0 + b * nblk + i, COL_CQ // gw + h)),
                  pl.BlockSpec((ns, C_HD), lambda b, h, i: (kb0 + b, COL_CK // C_HD + h)),
                  pl.BlockSpec((ns, C_HD), lambda b, h, i: (kb0 + b, COL_CV // C_HD + h)),
                  cache_spec, cache_spec,
                  pl.BlockSpec((w, LANES), lambda b, h, i: (i, 0)),
                  pl.BlockSpec((w, LANES), lambda b, h, i: (i, 0)),
                  full((ns, LANES)), full((ns, LANES)),
                  pl.BlockSpec(memory_space=pltpu.SMEM)],
        out_specs=pl.BlockSpec((w, gw), lambda b, h, i: (b * nblk + i, h)),
        scratch_shapes=[pltpu.VMEM((ns, C_HD), BF16), pltpu.VMEM((ns, C_HD), BF16)],
        compiler_params=_cparams("parallel", "parallel", "arbitrary"),
        name="sample_window_attn",
    )(proj, proj, proj, cache_k, cache_v, cos_t, sin_t, cos_t, sin_t, sink_l)


def _sample_fourier(proj, w2, csh, csl, wf, tp, nbatch, ns, tr):
    rpb = ns // tr
    kb0 = tp // ns

    def body(u_ref, w2_ref, csh_ref, csl_ref, wf_ref, o_ref, t_ref):
        @pl.when(pl.program_id(1) == 0)
        def _():
            for g in range(B_GROUPS):
                sl = slice(g * B_GD, (g + 1) * B_GD)
                t_ref[:, sl] = _fourier_stage1(u_ref[:, sl], csh_ref[...], csl_ref[...]).astype(BF16)

        f = _dot(w2_ref[...], t_ref[...])
        o_ref[...] = _dot(f.astype(BF16), wf_ref[...]).astype(BF16)

    full = lambda shape: pl.BlockSpec(shape, lambda b, r: (0,) * len(shape))
    return pl.pallas_call(
        body,
        out_shape=jax.ShapeDtypeStruct((nbatch * ns, GROUP_WIDTH), BF16),
        grid=(nbatch, rpb),
        in_specs=[pl.BlockSpec((ns, GROUP_WIDTH), lambda b, r: (kb0 + b, COL_BU // GROUP_WIDTH)),
                  pl.BlockSpec((tr, 2 * ns), lambda b, r: (r, 0)),
                  full(csh.shape), full(csl.shape), full(wf.shape)],
        out_specs=pl.BlockSpec((tr, GROUP_WIDTH), lambda b, r: (b * rpb + r, 0)),
        scratch_shapes=[pltpu.VMEM((2 * ns, GROUP_WIDTH), BF16)],
        compiler_params=_cparams("parallel", "arbitrary"),
        name="sample_fourier",
    )(proj, w2, csh, csl, wf)


def _spatial_gate(proj, g, b, ws, bst, tr):
    t = proj.shape[0]

    def body(u_ref, v_ref, g_ref, b_ref, ws_ref, bs_ref, o_ref):
        vn = (_ln_rows(v_ref[...]) * g_ref[...] + b_ref[...]).astype(BF16)
        for c in range(tr // CHUNK):
            rs = slice(c * CHUNK, (c + 1) * CHUNK)
            for h in range(D_HEADS):
                cs = slice(h * D_HD, (h + 1) * D_HD)
                mixed = _dot(ws_ref[h], vn[rs, cs]) + bs_ref[:, h:h + 1]
                o_ref[rs, cs] = (u_ref[rs, cs] * mixed).astype(BF16)

    full = lambda shape: pl.BlockSpec(shape, lambda i: (0,) * len(shape))
    return pl.pallas_call(
        body,
        out_shape=jax.ShapeDtypeStruct((t, GROUP_WIDTH), BF16),
        grid=(t // tr,),
        in_specs=[pl.BlockSpec((tr, GROUP_WIDTH), lambda i: (i, COL_DU // GROUP_WIDTH)),
                  pl.BlockSpec((tr, GROUP_WIDTH), lambda i: (i, COL_DV // GROUP_WIDTH)),
                  full(g.shape), full(b.shape), full(ws.shape), full(bst.shape)],
        out_specs=pl.BlockSpec((tr, GROUP_WIDTH), lambda i: (i, 0)),
        compiler_params=_cparams("parallel"),
        name="spatial_gate",
    )(proj, proj, g, b, ws, bst)


def _post_ln(x, f, gate, g, b):
    return _ln_rows(ALPHA * x + gate * f) * g + b


OUT_SUB_ROWS = 256
ROUTER_TERM_LANES = 8


def _pack_router(w_router_l, b_router_l):
    d = w_router_l.shape[0]
    assert N_EXPERTS <= ROUTER_TERM_LANES
    wp = jnp.zeros((d, LANES), BF16)
    for k, term in enumerate(_split3(w_router_l)):
        wp = wp.at[:, k * ROUTER_TERM_LANES:k * ROUTER_TERM_LANES + N_EXPERTS].set(term)
    br = jnp.zeros((1, LANES), F32).at[0, :N_EXPERTS].set(b_router_l)
    return wp, br


def _out_proj(abc_p, abc_s, od, w_all, layer, x, mods_l, g1, b1, router, tp, ns, tm):
    t, d = x.shape
    with_router = router is not None
    np_tiles = tp // tm
    sub = min(OUT_SUB_ROWS, tm)

    def body(ap_ref, bp_ref, cp_ref, as_ref, bs_ref, cs_ref, od_ref, w_ref, x_ref, m_ref,
             g_ref, b_ref, *rest):
        if with_router:
            wr_ref, br_ref, x1_ref, h2_ref, lg_ref, cat_ref = rest
        else:
            x1_ref, h2_ref, cat_ref = rest
        gw = GROUP_WIDTH
        is_ctx = pl.program_id(0) < np_tiles
        cat_ref[:, 0:gw] = jnp.where(is_ctx, ap_ref[...], as_ref[...])
        cat_ref[:, gw:2 * gw] = jnp.where(is_ctx, bp_ref[...], bs_ref[...])
        cat_ref[:, 2 * gw:3 * gw] = jnp.where(is_ctx, cp_ref[...], cs_ref[...])
        cat_ref[:, 3 * gw:4 * gw] = od_ref[...]
        for r in range(tm // sub):
            rs = slice(r * sub, (r + 1) * sub)
            mix = _dot(cat_ref[rs, :], w_ref[...])
            x1 = _post_ln(x_ref[rs, :], mix, m_ref[0, 2:3, :], g_ref[...], b_ref[...])
            x1_ref[rs, :] = x1
            h2 = _mod_ln(x1, m_ref, 3)
            h2_ref[rs, :] = h2.astype(h2_ref.dtype)
            if with_router:
                h0, h1, h2b = _split3(h2)
                p0, p1, p2 = (_dot(hk, wr_ref[...]) for hk in (h0, h1, h2b))
                left = lambda p, k: pltpu.roll(p, LANES - k * ROUTER_TERM_LANES, axis=1)
                lg = p0 + (left(p0, 1) + p1) + (left(p0, 2) + left(p1, 1) + p2)
                lg_ref[rs, :] = lg + br_ref[...]

    row = functools.partial(_mod_row, tm=tm, tp=tp, ns=ns)
    full = lambda shape: pl.BlockSpec(shape, lambda i: (0,) * len(shape))
    tile = lambda n: pl.BlockSpec((tm, n), lambda i: (i, 0))
    ctx_tile = pl.BlockSpec((tm, GROUP_WIDTH), lambda i: (jnp.minimum(i, np_tiles - 1), 0))
    lat_tile = pl.BlockSpec((tm, GROUP_WIDTH), lambda i: (jnp.maximum(i - np_tiles, 0), 0))
    w_spec = pl.BlockSpec((None,) + w_all.shape[1:], lambda i: (layer, 0, 0),
                          pipeline_mode=pl.Buffered(1))
    in_specs = [ctx_tile] * 3 + [lat_tile] * 3 + [
        tile(GROUP_WIDTH), w_spec, tile(d),
        pl.BlockSpec((1, 6, d), lambda i: (row(i), 0, 0)), full(g1.shape), full(b1.shape)]
    args = [*abc_p, *abc_s, od, w_all, x, mods_l, g1, b1]
    out_shape = [jax.ShapeDtypeStruct((t, d), F32),
                 jax.ShapeDtypeStruct((t, d), F32 if with_router else BF16)]
    out_specs = [tile(d), tile(d)]
    if with_router:
        in_specs += [full(r.shape) for r in router]
        args += list(router)
        out_shape.append(jax.ShapeDtypeStruct((t, LANES), F32))
        out_specs.append(tile(LANES))
    return pl.pallas_call(
        body, out_shape=tuple(out_shape), grid=(t // tm,),
        in_specs=in_specs, out_specs=tuple(out_specs),
        scratch_shapes=[pltpu.VMEM((tm, 4 * GROUP_WIDTH), BF16)],
        compiler_params=_cparams("parallel"),
        name="out_proj",
    )(*args)


def _ffn_up(h2, wg, wu, layer, moe_w, moe_layer, tm, tn):
    t, d = h2.shape
    dff = wg.shape[2]
    nj, ni = dff // tn, t // tm
    _, n_exp, _, dffe = moe_w[0].shape
    rows, cols = d // ni, dffe // nj
    assert rows * ni == d and cols * nj == dffe and rows % 16 == 0 and cols % LANES == 0

    def body(h_ref, wg_ref, wu_ref, mg_ref, mu_ref, md_ref, a_ref, og_ref, ou_ref, od_ref):
        h = h_ref[...]
        a_ref[...] = (_silu(_dot(h, wg_ref[...])) * _dot(h, wu_ref[...])).astype(BF16)
        for src, dst in ((mg_ref, og_ref), (mu_ref, ou_ref), (md_ref, od_ref)):
            dst[...] = src[...].astype(BF16)

    up_in = pl.BlockSpec((None, n_exp, rows, cols), lambda j, i: (moe_layer, 0, i, j))
    up_out = pl.BlockSpec((n_exp, rows, cols), lambda j, i: (0, i, j))
    down_in = pl.BlockSpec((None, n_exp, cols, rows), lambda j, i: (moe_layer, 0, j, i))
    down_out = pl.BlockSpec((n_exp, cols, rows), lambda j, i: (0, j, i))
    return pl.pallas_call(
        body,
        out_shape=(jax.ShapeDtypeStruct((t, dff), BF16),
                   jax.ShapeDtypeStruct((n_exp, d, dffe), BF16),
                   jax.ShapeDtypeStruct((n_exp, d, dffe), BF16),
                   jax.ShapeDtypeStruct((n_exp, dffe, d), BF16)),
        grid=(nj, ni),
        in_specs=[pl.BlockSpec((tm, d), lambda j, i: (i, 0)),
                  pl.BlockSpec((None, d, tn), lambda j, i: (layer, 0, j)),
                  pl.BlockSpec((None, d, tn), lambda j, i: (layer, 0, j)),
                  up_in, up_in, down_in],
        out_specs=(pl.BlockSpec((tm, tn), lambda j, i: (i, j)), up_out, up_out, down_out),
        compiler_params=_cparams("parallel", "parallel"),
        name="ffn_up",
    )(h2, wg, wu, *moe_w)


def _ffn_down(a, wd, layer, x1, mods_l, mods_next, g2, b2, tp, ns, tm):
    t, d = x1.shape
    dff = a.shape[1]

    def body(a_ref, wd_ref, x_ref, m_ref, mn_ref, g_ref, b_ref, o_ref, hn_ref):
        f = _dot(a_ref[...], wd_ref[...])
        x2 = _post_ln(x_ref[...], f, m_ref[0, 5:6, :], g_ref[...], b_ref[...])
        o_ref[...] = x2
        hn_ref[...] = _mod_ln(x2, mn_ref, 0).astype(BF16)

    row = functools.partial(_mod_row, tm=tm, tp=tp, ns=ns)
    full = lambda shape: pl.BlockSpec(shape, lambda i: (0,) * len(shape))
    mod_spec = pl.BlockSpec((1, 6, d), lambda i: (row(i), 0, 0))
    tile = pl.BlockSpec((tm, d), lambda i: (i, 0))
    return pl.pallas_call(
        body,
        out_shape=(jax.ShapeDtypeStruct((t, d), F32), jax.ShapeDtypeStruct((t, d), BF16)),
        grid=(t // tm,),
        in_specs=[pl.BlockSpec((tm, dff), lambda i: (i, 0)),
                  pl.BlockSpec((None, dff, d), lambda i: (layer, 0, 0), pipeline_mode=pl.Buffered(1)),
                  tile, mod_spec, mod_spec, full(g2.shape), full(b2.shape)],
        out_specs=(tile, tile),
        compiler_params=_cparams("parallel"),
        name="ffn_down",
    )(a, wd, x1, mods_l, mods_next, g2, b2)


ROUTE_I1, ROUTE_I2, ROUTE_W1, ROUTE_W2, ROUTE_R1, ROUTE_R2 = range(6)


def _route(logits, tr):
    t = logits.shape[0]

    def body(lg_ref, route_ref, cnt_ref, carry_ref):
        @pl.when(pl.program_id(0) == 0)
        def _():
            carry_ref[...] = jnp.zeros_like(carry_ref)

        lane = lax.broadcasted_iota(jnp.int32, (tr, LANES), 1)
        lane_f = lane.astype(F32)
        lg = jnp.where(lane < N_EXPERTS, lg_ref[...], -jnp.inf)
        m1 = jnp.max(lg, axis=-1, keepdims=True)
        i1 = jnp.min(jnp.where(lg == m1, lane_f, float(LANES)), axis=-1, keepdims=True)
        lg2 = jnp.where(lane_f == i1, -jnp.inf, lg)
        m2 = jnp.max(lg2, axis=-1, keepdims=True)
        i2 = jnp.min(jnp.where(lg2 == m2, lane_f, float(LANES)), axis=-1, keepdims=True)
        e2 = jnp.exp(m2 - m1)
        den = 1.0 + e2
        oh1 = lane_f == i1
        oh2 = lane_f == i2
        sel = jnp.where(oh1 | oh2, 1.0, 0.0)
        ri = lax.broadcasted_iota(jnp.int32, (tr, tr), 0)
        ci = lax.broadcasted_iota(jnp.int32, (tr, tr), 1)
        tri = jnp.where(ri > ci, 1.0, 0.0).astype(BF16)
        prefix = _dot(tri, sel.astype(BF16)) + carry_ref[0:1, :]
        r1 = jnp.sum(jnp.where(oh1, prefix, 0.0), axis=-1, keepdims=True)
        r2 = jnp.sum(jnp.where(oh2, prefix, 0.0), axis=-1, keepdims=True)
        carry_ref[0:1, :] = carry_ref[0:1, :] + jnp.sum(sel, axis=0, keepdims=True)
        out = jnp.zeros((tr, LANES), F32)
        for idx, val in ((ROUTE_I1, i1), (ROUTE_I2, i2), (ROUTE_W1, 1.0 / den),
                         (ROUTE_W2, e2 / den), (ROUTE_R1, r1), (ROUTE_R2, r2)):
            out = jnp.where(lane == idx, val, out)
        route_ref[...] = out
        cnt_ref[...] = carry_ref[...]

    return pl.pallas_call(
        body,
        out_shape=(jax.ShapeDtypeStruct((t, LANES), F32), jax.ShapeDtypeStruct((8, LANES), F32)),
        grid=(t // tr,),
        in_specs=[pl.BlockSpec((tr, LANES), lambda i: (i, 0))],
        out_specs=(pl.BlockSpec((tr, LANES), lambda i: (i, 0)),
                   pl.BlockSpec((8, LANES), lambda i: (0, 0))),
        scratch_shapes=[pltpu.VMEM((8, LANES), F32)],
        compiler_params=_cparams("arbitrary"),
        name="route",
    )(logits)


ROW_COPY_UNROLL = 8


def _row_sources(pos1, pos2, fill_from, fill_to, n_rows):
    t = pos1.shape[0]

    def body(p1_ref, p2_ref, lo_ref, hi_ref, src_ref):
        def clear(r, carry):
            src_ref[r] = 0
            return carry

        def put(tok, carry):
            src_ref[p1_ref[tok]] = tok
            src_ref[p2_ref[tok]] = tok
            return carry

        lax.fori_loop(0, t, put, 0, unroll=ROW_COPY_UNROLL)
        for e in range(N_EXPERTS):
            lax.fori_loop(lo_ref[e], hi_ref[e], clear, 0)
        lax.fori_loop(hi_ref[N_EXPERTS - 1], n_rows, clear, 0)

    smem = pl.BlockSpec(memory_space=pltpu.SMEM)
    return pl.pallas_call(
        body,
        out_shape=jax.ShapeDtypeStruct((n_rows,), jnp.int32),
        in_specs=[smem] * 4, out_specs=smem,
        name="row_sources",
    )(pos1, pos2, fill_from, fill_to)


def _experts(tile_expert, n_used, src, h, wg, wu, wd, tm, ck):
    d = h.shape[1]
    nt = src.shape[0] // tm
    dff = wg.shape[2]

    def row_copy(src_ref, h_ref, xg_ref, sem, tile, slot, r):
        return pltpu.make_async_copy(h_ref.at[pl.ds(src_ref[tile * tm + r], 1)],
                                     xg_ref.at[slot, pl.ds(r, 1)], sem.at[slot])

    def body(te_ref, nu_ref, src_ref, h_ref, wg_ref, wu_ref, wd_ref, y_ref, xg_ref, a_ref, sem):
        del te_ref
        i = pl.program_id(0)
        nu = nu_ref[0]

        def issue(tile, slot):
            def step(r, carry):
                row_copy(src_ref, h_ref, xg_ref, sem, tile, slot, r).start()
                return carry
            lax.fori_loop(0, tm, step, 0, unroll=ROW_COPY_UNROLL)

        def drain(tile, slot):
            del tile
            pltpu.make_async_copy(h_ref.at[pl.ds(0, tm)], xg_ref.at[slot], sem.at[slot]).wait()

        @pl.when(i == 0)
        def _():
            issue(0, 0)

        @pl.when(i < nu)
        def _():
            slot = i & 1
            drain(i, slot)
            x = xg_ref[slot].astype(BF16)
            nxt = jnp.minimum(i + 1, nu - 1)
            nc = dff // ck
            for c in range(nc):
                cs = slice(c * ck, (c + 1) * ck)
                a_ref[:, cs] = (_silu(_dot(x, wg_ref[:, cs])) * _dot(x, wu_ref[:, cs])).astype(BF16)
                for r in range(c * tm // nc, (c + 1) * tm // nc):
                    row_copy(src_ref, h_ref, xg_ref, sem, nxt, 1 - slot, r).start()
            y_ref[...] = _dot(a_ref[...], wd_ref[...])

            @pl.when(i == nu - 1)
            def _():
                drain(nxt, 1 - slot)

        @pl.when(i >= nu)
        def _():
            y_ref[...] = jnp.zeros_like(y_ref)

    def expert_weights(shape):
        return pl.BlockSpec((None,) + shape, lambda i, te, nu, src: (te[i], 0, 0),
                            pipeline_mode=pl.Buffered(1))

    return pl.pallas_call(
        body,
        out_shape=jax.ShapeDtypeStruct((nt * tm, d), F32),
        grid_spec=pltpu.PrefetchScalarGridSpec(
            num_scalar_prefetch=3, grid=(nt,),
            in_specs=[pl.BlockSpec(memory_space=pl.ANY),
                      expert_weights((d, dff)), expert_weights((d, dff)),
                      expert_weights((dff, d))],
            out_specs=pl.BlockSpec((tm, d), lambda i, te, nu, src: (i, 0)),
            scratch_shapes=[pltpu.VMEM((2, tm, d), F32), pltpu.VMEM((tm, dff), BF16),
                            pltpu.SemaphoreType.DMA((2,))]),
        compiler_params=_cparams("arbitrary"),
        name="experts",
    )(tile_expert, n_used, src, h, wg, wu, wd)


COMBINE_ROWS = 32


def _combine(pos1, pos2, ys, route, x1, mods_l, mods_next, g2, b2, tp, ns, tc):
    t, d = x1.shape
    is_last = mods_next is None
    np_tiles = tp // tc

    def row_copies(p1_ref, p2_ref, ys_ref, y1_ref, y2_ref, sem, tile, slot, r):
        tok = tile * tc + r
        return (pltpu.make_async_copy(ys_ref.at[pl.ds(p1_ref[tok], 1)],
                                      y1_ref.at[slot, pl.ds(r, 1)], sem.at[0, slot]),
                pltpu.make_async_copy(ys_ref.at[pl.ds(p2_ref[tok], 1)],
                                      y2_ref.at[slot, pl.ds(r, 1)], sem.at[1, slot]))

    def body(p1_ref, p2_ref, ys_ref, rt_ref, x_ref, m_ref, *rest):
        if is_last:
            g_ref, b_ref, yp_ref, yl_ref, y1_ref, y2_ref, sem, o_ref = rest
        else:
            mn_ref, g_ref, b_ref, o_ref, hn_ref, y1_ref, y2_ref, sem = rest
        j = pl.program_id(0)

        def issue(tile, slot):
            def step(r, carry):
                for cp in row_copies(p1_ref, p2_ref, ys_ref, y1_ref, y2_ref, sem, tile, slot, r):
                    cp.start()
                return carry
            lax.fori_loop(0, tc, step, 0, unroll=ROW_COPY_UNROLL)

        def drain(slot):
            for k, buf in enumerate((y1_ref, y2_ref)):
                pltpu.make_async_copy(ys_ref.at[pl.ds(0, tc)], buf.at[slot], sem.at[k, slot]).wait()

        @pl.when(j == 0)
        def _():
            issue(0, 0)

        slot = j & 1
        drain(slot)
        last = pl.num_programs(0) - 1
        nxt = jnp.minimum(j + 1, last)
        for blk in range(tc // COMBINE_ROWS):
            rs = slice(blk * COMBINE_ROWS, (blk + 1) * COMBINE_ROWS)
            rt = rt_ref[rs, :]
            f = (rt[:, ROUTE_W1:ROUTE_W1 + 1] * y1_ref[slot, rs, :]
                 + rt[:, ROUTE_W2:ROUTE_W2 + 1] * y2_ref[slot, rs, :])
            x2 = _post_ln(x_ref[rs, :], f, m_ref[0, 5:6, :], g_ref[...], b_ref[...])
            o_ref[rs, :] = x2
            if not is_last:
                hn_ref[rs, :] = _mod_ln(x2, mn_ref, 0).astype(BF16)
            for r in range(blk * COMBINE_ROWS, (blk + 1) * COMBINE_ROWS):
                for cp in row_copies(p1_ref, p2_ref, ys_ref, y1_ref, y2_ref, sem, nxt, 1 - slot, r):
                    cp.start()

        @pl.when(j == last)
        def _():
            drain(1 - slot)

        if is_last:
            @pl.when(j < np_tiles)
            def _():
                yp_ref[...] = o_ref[...]

            @pl.when(j >= np_tiles)
            def _():
                yl_ref[...] = o_ref[...]

    row = functools.partial(_mod_row, tm=tc, tp=tp, ns=ns)
    full = lambda shape: pl.BlockSpec(shape, lambda i, p1, p2: (0,) * len(shape))
    tile = pl.BlockSpec((tc, d), lambda i, p1, p2: (i, 0))
    mod_spec = pl.BlockSpec((1, 6, d), lambda i, p1, p2: (row(i), 0, 0))
    in_specs = [pl.BlockSpec(memory_space=pl.ANY),
                pl.BlockSpec((tc, LANES), lambda i, p1, p2: (i, 0)), tile, mod_spec]
    args = [pos1, pos2, ys, route, x1, mods_l]
    scratch = [pltpu.VMEM((2, tc, d), F32), pltpu.VMEM((2, tc, d), F32),
               pltpu.SemaphoreType.DMA((2, 2))]
    if is_last:
        out_shape = (jax.ShapeDtypeStruct((tp, d), F32), jax.ShapeDtypeStruct((t - tp, d), F32))
        out_specs = (pl.BlockSpec((tc, d), lambda i, p1, p2: (jnp.minimum(i, np_tiles - 1), 0)),
                     pl.BlockSpec((tc, d), lambda i, p1, p2: (jnp.maximum(i - np_tiles, 0), 0)))
        scratch.append(pltpu.VMEM((tc, d), F32))
    else:
        in_specs.append(mod_spec)
        args.append(mods_next)
        out_shape = (jax.ShapeDtypeStruct((t, d), F32), jax.ShapeDtypeStruct((t, d), BF16))
        out_specs = (tile, tile)
    in_specs += [full(g2.shape), full(b2.shape)]
    args += [g2, b2]
    return pl.pallas_call(
        body,
        out_shape=out_shape,
        grid_spec=pltpu.PrefetchScalarGridSpec(
            num_scalar_prefetch=2, grid=(t // tc,),
            in_specs=in_specs, out_specs=out_specs, scratch_shapes=scratch),
        compiler_params=_cparams("arbitrary"),
        name="combine",
    )(*args)


def _ffn_routed(h2, logits, wg, wu, wd, x1, mods_l, mods_next, g2, b2, tp, ns, tm, ck):
    t = h2.shape[0]
    route, cnt = _route(logits, min(512, t))
    i1 = route[:, ROUTE_I1].astype(jnp.int32)
    i2 = route[:, ROUTE_I2].astype(jnp.int32)
    counts = cnt[0, :N_EXPERTS].astype(jnp.int32)
    padded = (counts + tm - 1) // tm * tm
    ends = jnp.cumsum(padded)
    starts = ends - padded
    pos1 = starts[i1] + route[:, ROUTE_R1].astype(jnp.int32)
    pos2 = starts[i2] + route[:, ROUTE_R2].astype(jnp.int32)
    nt = 2 * t // tm + N_EXPERTS
    n_used = ends[-1] // tm
    tile_start = jnp.arange(nt, dtype=jnp.int32) * tm
    tile_expert = jnp.sum(tile_start[:, None] >= ends[None, :], axis=1).astype(jnp.int32)
    tile_expert = jnp.minimum(tile_expert, tile_expert[n_used - 1])
    src = _row_sources(pos1, pos2, starts + counts, ends, nt * tm)
    ys = _experts(tile_expert, n_used.reshape(1), src, h2, wg, wu, wd, tm, ck)
    return _combine(pos1, pos2, ys, route, x1, mods_l, mods_next, g2, b2, tp, ns, tm)


def _permute_in_cols(w):
    d = w.shape[1]
    qk = w[..., :4 * A_HEADS * A_QK].reshape(w.shape[0], d, 2, 2, A_HEADS, A_QK)
    qk = qk.transpose(0, 1, 2, 4, 3, 5).reshape(w.shape[0], d, 4 * A_HEADS * A_QK)
    return jnp.concatenate([qk, w[..., 4 * A_HEADS * A_QK:]], axis=-1)


def kernel(x_prompt, x_sample, cache_a_k, cache_a_v, cache_c_k, cache_c_v, c, c_ctx, w_ada, b_ada, w_in, lam, a_subln, w_fnet, c_sink, d_norm_g, d_norm_b, d_ws, d_bs, w_out, ln1_g, ln1_b, ln2_g, ln2_b, ffn_wg, ffn_wu, ffn_wd, w_router, b_router, moe_wg, moe_wu, moe_wd):
    nb_p, n_p, d = x_prompt.shape
    nb_s, n_s, _ = x_sample.shape
    depth = w_in.shape[0]
    past = cache_a_k.shape[2]
    tp = nb_p * n_p
    t = tp + nb_s * n_s
    assert tp % n_s == 0 and n_s % 512 == 0 and n_p % CHUNK == 0 and nb_s + 1 <= 8

    assert depth % 2 == 0
    c_all = jnp.zeros((8, d), F32).at[0].set(c_ctx).at[1:1 + nb_s].set(c)
    mods = _modulation(c_all, w_ada, b_ada).reshape(depth, 8, 6, d)

    w_in_b = _permute_in_cols(w_in.astype(BF16))
    w_out_b = w_out.astype(BF16)
    w_fnet_b = w_fnet.astype(BF16)
    d_ws_b = d_ws.astype(BF16)
    ffn_wg_b, ffn_wu_b, ffn_wd_b = (w.astype(BF16) for w in (ffn_wg, ffn_wu, ffn_wd))
    rope_a = _rope_tables(n_s, A_QK)
    rope_c = _rope_tables(n_s, C_HD)
    csh, csl = _split(jnp.asarray(_dft_channel_table()))
    w2p_h = jnp.asarray(_dft_table(n_p)).astype(BF16)
    w2s_h = jnp.asarray(_dft_table(n_s)).astype(BF16)
    ca_k = cache_a_k.reshape(nb_s, depth, past, A_HEADS * A_V)
    ca_v = cache_a_v.reshape(nb_s, depth, past, A_HEADS * A_V)
    cc_k = cache_c_k.reshape(nb_s, depth, past, C_KV_HEADS * C_HD)
    cc_v = cache_c_v.reshape(nb_s, depth, past, C_KV_HEADS * C_HD)

    tm_big = math.gcd(1024, tp, n_s)
    tm_mid = math.gcd(512, tp, n_s)
    tm_small = math.gcd(256, tp, n_s)
    caches = tuple(jnp.zeros((nb_p, depth, n_p, cut[2]), F32) for cut in CACHE_CUTS)
    x, h = _first_ln(x_prompt.reshape(tp, d), x_sample.reshape(nb_s * n_s, d), mods[0], n_s, tm_mid)
    for l in range(depth):
        lam_init = 0.8 - 0.6 * math.exp(-0.3 * l)
        mods_next = mods[l + 1] if l + 1 < depth else None
        proj, *caches = _in_proj(h, w_in_b, l, caches, tp, tm_big)

        subln = a_subln[l].reshape(1, A_V)
        abc_p = _prompt_mixers(proj, lam[l], subln, c_sink[l], w2p_h, csh, csl,
                               w_fnet_b[l], lam_init, nb_p, n_p)
        oa_s = _sample_diff_attn(proj, ca_k, ca_v, l, rope_a[0], rope_a[1], lam[l], subln,
                                 lam_init, tp, nb_s, n_s, 256)
        oc_s = _sample_window_attn(proj, cc_k, cc_v, l, rope_c[0], rope_c[1], c_sink[l],
                                   tp, nb_s, n_s)
        ob_s = _sample_fourier(proj, w2s_h, csh, csl, w_fnet_b[l], tp, nb_s, n_s, tm_mid)
        abc_s = (oa_s, ob_s, oc_s)
        od = _spatial_gate(proj, d_norm_g[l].reshape(1, -1), d_norm_b[l].reshape(1, -1),
                           d_ws_b[l], d_bs[l].T, 256)

        g1, b1 = ln1_g[l].reshape(1, d), ln1_b[l].reshape(1, d)
        g2, b2 = ln2_g[l].reshape(1, d), ln2_b[l].reshape(1, d)
        j = l // 2
        if l % 2 == 0:
            x1, h2 = _out_proj(abc_p, abc_s, od, w_out_b, l, x, mods[l], g1, b1, None,
                               tp, n_s, tm_mid)
            a, *moe_b = _ffn_up(h2, ffn_wg_b, ffn_wu_b, j, (moe_wg, moe_wu, moe_wd), j, tm_big, 512)
            x, h = _ffn_down(a, ffn_wd_b, j, x1, mods[l], mods_next, g2, b2, tp, n_s, tm_small)
        else:
            router = _pack_router(w_router[j], b_router[j])
            x1, h2, logits = _out_proj(abc_p, abc_s, od, w_out_b, l, x, mods[l], g1, b1, router,
                                       tp, n_s, tm_mid)
            x, h = _ffn_routed(h2, logits, *moe_b, x1, mods[l], mods_next,
                               g2, b2, tp, n_s, tm_small, 256)

    y_p = x.reshape(nb_p, n_p, d)
    y_s = h.reshape(nb_s, n_s, d)
    ak, av, ck, cv = caches
    return (y_p, y_s,
            ak.reshape(nb_p, depth, n_p, A_HEADS, 2 * A_QK), av.reshape(nb_p, depth, n_p, A_HEADS, A_V),
            ck.reshape(nb_p, depth, n_p, C_KV_HEADS, C_HD), cv.reshape(nb_p, depth, n_p, C_KV_HEADS, C_HD))
```

```python
import functools
import math

import jax
import jax.numpy as jnp
import numpy as np
from jax import lax
from jax.experimental import pallas as pl
from jax.experimental.pallas import tpu as pltpu

F32 = jnp.float32
BF16 = jnp.bfloat16

GRID_W = 64
A_V = 128
A_QK = 64
A_HEADS = 4
B_GD = 128
B_GROUPS = 4
C_HD = 128
C_HEADS = 4
C_KV_HEADS = 2
C_GROUP = C_HEADS // C_KV_HEADS
C_WINDOW = 128
D_HD = 128
D_HEADS = 4
CHUNK = 128
GROUP_WIDTH = 512
N_EXPERTS = 8
ROPE_BASE = 10000.0
LN_EPS = 1e-6
SUBLN_EPS = 1e-5
NEG = -1e30
TOTAL_DEPTH = 4
ALPHA = (2.0 * TOTAL_DEPTH) ** 0.25

COL_AQ = 0
COL_AK = 512
COL_AV = 1024
COL_BU = 1536
COL_CQ = 2048
COL_CK = 2560
COL_CV = 2816
COL_DU = 3072
COL_DV = 3584

LANES = 128
V7X_VMEM_BYTES = 64 * 1024 * 1024
VMEM_LIMIT = V7X_VMEM_BYTES - 8 * 1024 * 1024


def _cparams(*sem):
    return pltpu.CompilerParams(dimension_semantics=sem, vmem_limit_bytes=VMEM_LIMIT)


def _dot(a, b):
    return jnp.dot(a, b, preferred_element_type=F32)


def _dot_nt(a, b):
    return lax.dot_general(a, b, (((1,), (1,)), ((), ())), preferred_element_type=F32)


def _split(a):
    hi = a.astype(BF16)
    lo = (a - hi.astype(F32)).astype(BF16)
    return hi, lo


def _split3(a):
    a0 = a.astype(BF16)
    r1 = a - a0.astype(F32)
    a1 = r1.astype(BF16)
    a2 = (r1 - a1.astype(F32)).astype(BF16)
    return a0, a1, a2


def _dot3(a_hi, a_lo, b_hi, b_lo):
    return _dot(a_hi, b_hi) + (_dot(a_hi, b_lo) + _dot(a_lo, b_hi))


def _ln_rows(x):
    xc = x - jnp.mean(x, axis=-1, keepdims=True)
    return xc * lax.rsqrt(jnp.mean(xc * xc, axis=-1, keepdims=True) + LN_EPS)


def _silu(x):
    return x / (1.0 + jnp.exp(-x))


def _mod_row(i, tm, tp, ns):
    return jnp.where(i * tm < tp, 0, 1 + (i * tm - tp) // ns)


def _modulation(c_all, w_ada, b_ada):
    depth, d, n6 = w_ada.shape
    rows = c_all.shape[0]
    tn = 512

    def body(c_ref, w_ref, b_ref, o_ref):
        s = _silu(c_ref[...]).astype(BF16)
        o_ref[0] = _dot(s, w_ref[0].astype(BF16)) + b_ref[0]

    return pl.pallas_call(
        body,
        out_shape=jax.ShapeDtypeStruct((depth, rows, n6), F32),
        grid=(depth, n6 // tn),
        in_specs=[pl.BlockSpec((rows, d), lambda l, j: (0, 0)),
                  pl.BlockSpec((1, d, tn), lambda l, j: (l, 0, j)),
                  pl.BlockSpec((1, 1, tn), lambda l, j: (l, 0, j))],
        out_specs=pl.BlockSpec((1, rows, tn), lambda l, j: (l, 0, j)),
        compiler_params=_cparams("parallel", "parallel"),
        name="modulation",
    )(c_all, w_ada, b_ada.reshape(depth, 1, n6))


def _mod_ln(x, m_ref, k):
    return _ln_rows(x) * (1.0 + m_ref[0, k + 1:k + 2, :]) + m_ref[0, k:k + 1, :]


def _first_ln(x_p, x_s, mods_l, ns, tm):
    tp, d = x_p.shape
    t = tp + x_s.shape[0]
    np_tiles = tp // tm

    def body(xp_ref, xs_ref, m_ref, x_ref, h_ref):
        x = jnp.where(pl.program_id(0) < np_tiles, xp_ref[...], xs_ref[...])
        x_ref[...] = x
        h_ref[...] = _mod_ln(x, m_ref, 0).astype(BF16)

    row = functools.partial(_mod_row, tm=tm, tp=tp, ns=ns)
    tile = pl.BlockSpec((tm, d), lambda i: (i, 0))
    return pl.pallas_call(
        body,
        out_shape=(jax.ShapeDtypeStruct((t, d), F32), jax.ShapeDtypeStruct((t, d), BF16)),
        grid=(t // tm,),
        in_specs=[pl.BlockSpec((tm, d), lambda i: (jnp.minimum(i, np_tiles - 1), 0)),
                  pl.BlockSpec((tm, d), lambda i: (jnp.maximum(i - np_tiles, 0), 0)),
                  pl.BlockSpec((1, 6, d), lambda i: (row(i), 0, 0))],
        out_specs=(tile, tile),
        compiler_params=_cparams("parallel"),
        name="first_ln",
    )(x_p, x_s, mods_l)


IN_TN = 1024
CACHE_CUTS = ((COL_AK // IN_TN, COL_AK % IN_TN, COL_AV - COL_AK),
              (COL_AV // IN_TN, COL_AV % IN_TN, COL_BU - COL_AV),
              (COL_CK // IN_TN, COL_CK % IN_TN, COL_CV - COL_CK),
              (COL_CV // IN_TN, COL_CV % IN_TN, COL_DU - COL_CV))


def _in_proj(h, w_all, layer, caches, tp, tm):
    t, d = h.shape
    n = w_all.shape[2]
    tn = IN_TN
    np_tiles = tp // tm
    n_p = caches[0].shape[2]
    bpt = tm // n_p
    assert bpt * n_p == tm

    def body(h_ref, w_ref, *rest):
        o_ref = rest[len(caches)]
        cache_refs = rest[len(caches) + 1:]
        i = pl.program_id(0)
        j = pl.program_id(1)
        o_ref[...] = _dot(h_ref[...], w_ref[...])

        for c_ref, (jt, c0, width) in zip(cache_refs, CACHE_CUTS):
            @pl.when(jnp.logical_and(j == jt, i < np_tiles))
            def _(c_ref=c_ref, c0=c0, width=width):
                for bb in range(bpt):
                    c_ref[bb] = o_ref[bb * n_p:(bb + 1) * n_p, c0:c0 + width]

    def cache_spec(c):
        return pl.BlockSpec((bpt, None, n_p, c.shape[3]),
                            lambda i, j: (jnp.minimum(i, np_tiles - 1), layer, 0, 0))

    n_in = 2
    return pl.pallas_call(
        body,
        out_shape=(jax.ShapeDtypeStruct((t, n), F32),
                   *[jax.ShapeDtypeStruct(c.shape, c.dtype) for c in caches]),
        grid=(t // tm, n // tn),
        in_specs=[pl.BlockSpec((tm, d), lambda i, j: (i, 0)),
                  pl.BlockSpec((None, d, tn), lambda i, j: (layer, 0, j)),
                  *[pl.BlockSpec(memory_space=pl.ANY) for _ in caches]],
        out_specs=(pl.BlockSpec((tm, tn), lambda i, j: (i, j)), *[cache_spec(c) for c in caches]),
        input_output_aliases={n_in + k: 1 + k for k in range(len(caches))},
        compiler_params=_cparams("parallel", "arbitrary"),
        name="in_proj",
    )(h, w_all, *caches)


def _lam_value(lam_ref, lam_init):
    lv = lam_ref[...]
    d1 = jnp.sum(lv[0:1] * lv[1:2], axis=-1, keepdims=True)
    d2 = jnp.sum(lv[2:3] * lv[3:4], axis=-1, keepdims=True)
    return jnp.exp(d1) - jnp.exp(d2) + lam_init


def _softmax(s):
    e = jnp.exp(s - jnp.max(s, axis=-1, keepdims=True))
    return e / jnp.sum(e, axis=-1, keepdims=True)


def _diff_attend(q, kb, vb, lam):
    lane = lax.broadcasted_iota(jnp.int32, q.shape, 1)
    q1 = jnp.where(lane < A_QK, q, 0.0).astype(BF16)
    q2 = jnp.where(lane >= A_QK, q, 0.0).astype(BF16)
    scale = A_QK ** -0.5
    s1 = _dot_nt(q1, kb) * scale
    s2 = _dot_nt(q2, kb) * scale
    p = _softmax(s1) - lam * _softmax(s2)
    return _dot(p.astype(BF16), vb)


KEY_CHUNK = 256


def _diff_attend_chunked(q, kr_ref, vb_ref, lam, kc):
    lane = lax.broadcasted_iota(jnp.int32, q.shape, 1)
    qs = q * (A_QK ** -0.5)
    q1 = jnp.where(lane < A_QK, qs, 0.0).astype(BF16)
    q2 = jnp.where(lane >= A_QK, qs, 0.0).astype(BF16)
    chunks = [slice(c * kc, (c + 1) * kc) for c in range(kr_ref.shape[0] // kc)]

    def fold(x, op):
        out = x[:, :LANES]
        for j in range(1, kc // LANES):
            out = op(out, x[:, j * LANES:(j + 1) * LANES])
        return out

    m1 = m2 = None
    for cs in chunks:
        kb = kr_ref[cs, :]
        c1 = fold(_dot_nt(q1, kb), jnp.maximum)
        c2 = fold(_dot_nt(q2, kb), jnp.maximum)
        m1 = c1 if m1 is None else jnp.maximum(m1, c1)
        m2 = c2 if m2 is None else jnp.maximum(m2, c2)
    m1 = jnp.max(m1, axis=-1, keepdims=True)
    m2 = jnp.max(m2, axis=-1, keepdims=True)
    l1 = l2 = o1 = o2 = 0.0
    for cs in chunks:
        kb = kr_ref[cs, :]
        vb = vb_ref[cs, :]
        e1 = jnp.exp(_dot_nt(q1, kb) - m1)
        e2 = jnp.exp(_dot_nt(q2, kb) - m2)
        l1 = l1 + fold(e1, jnp.add)
        l2 = l2 + fold(e2, jnp.add)
        o1 = o1 + _dot(e1.astype(BF16), vb)
        o2 = o2 + _dot(e2.astype(BF16), vb)
    l1 = jnp.sum(l1, axis=-1, keepdims=True)
    l2 = jnp.sum(l2, axis=-1, keepdims=True)
    return o1 / l1 - lam * (o2 / l2)


def _sub_ln(o, g, lam_init):
    r = lax.rsqrt(jnp.mean(o * o, axis=-1, keepdims=True) + SUBLN_EPS)
    return (o * r) * g * (1.0 - lam_init)


def _rope(x, cos, sin_signed, half):
    if 2 * half == LANES:
        swapped = pltpu.roll(x, half, axis=1)
    else:
        lane = lax.broadcasted_iota(jnp.int32, x.shape, 1)
        first = (lane % (2 * half)) < half
        swapped = jnp.where(first, pltpu.roll(x, LANES - half, axis=1),
                            pltpu.roll(x, half, axis=1))
    return x * cos + swapped * sin_signed


def _rope_tables(n_tokens, head_dim):
    rows = n_tokens // GRID_W
    row = np.repeat(np.arange(rows, dtype=np.float64), GRID_W)
    col = np.tile(np.arange(GRID_W, dtype=np.float64), rows)
    n_freq = head_dim // 4
    inv = np.power(ROPE_BASE, -np.arange(n_freq, dtype=np.float64) / n_freq)
    ang = np.concatenate([row[:, None] * inv, col[:, None] * inv], axis=-1)
    cos, sin = np.cos(ang), np.sin(ang)
    reps = LANES // head_dim
    cos_t = np.tile(np.concatenate([cos, cos], axis=-1), (1, reps))
    sin_t = np.tile(np.concatenate([-sin, sin], axis=-1), (1, reps))
    return cos_t.astype(np.float32), sin_t.astype(np.float32)


def _dft_cos_sin(n):
    jk = np.outer(np.arange(n), np.arange(n)) % n
    ang = jk.astype(np.float64) * (2.0 * math.pi / n)
    return np.cos(ang) * n ** -0.5, np.sin(ang) * n ** -0.5


def _dft_table(n):
    c, s = _dft_cos_sin(n)
    return np.concatenate([c, -s], axis=1).astype(np.float32)


def _dft_channel_table():
    c, s = _dft_cos_sin(B_GD)
    return np.concatenate([c, s], axis=1).astype(np.float32)


def _fourier_stage1(u, csh, csl):
    uh, ul = _split(u)
    t = _dot3(uh, ul, csh, csl)
    return jnp.concatenate([t[:, :B_GD], t[:, B_GD:]], axis=0)


PROMPT_BATCHES_PER_STEP = 2


def _prompt_mixers(proj, lam_l, subln_l, sink_l, w2, csh, csl, wf, lam_init, nbatch, n):
    t = nbatch * n
    bps = math.gcd(PROMPT_BATCHES_PER_STEP, nbatch)

    def body(p_ref, lam_ref, g_ref, sink_ref, w2_ref, csh_ref, csl_ref, wf_ref,
             oa_ref, ob_ref, oc_ref):
        lam = _lam_value(lam_ref, lam_init)
        scale = C_HD ** -0.5
        for bb in range(bps):
            rows = slice(bb * n, (bb + 1) * n)
            for h in range(A_HEADS):
                q = p_ref[rows, COL_AQ + h * A_V:COL_AQ + (h + 1) * A_V]
                kb = p_ref[rows, COL_AK + h * A_V:COL_AK + (h + 1) * A_V].astype(BF16)
                vb = p_ref[rows, COL_AV + h * A_V:COL_AV + (h + 1) * A_V].astype(BF16)
                o = _diff_attend(q, kb, vb, lam)
                oa_ref[rows, h * A_V:(h + 1) * A_V] = _sub_ln(o, g_ref[...], lam_init).astype(BF16)

            acc = None
            for g in range(B_GROUPS):
                u = p_ref[rows, COL_BU + g * B_GD:COL_BU + (g + 1) * B_GD]
                tb = _fourier_stage1(u, csh_ref[...], csl_ref[...]).astype(BF16)
                f = _dot(w2_ref[...], tb)
                part = _dot(f.astype(BF16), wf_ref[g * B_GD:(g + 1) * B_GD, :])
                acc = part if acc is None else acc + part
            ob_ref[rows, :] = acc.astype(BF16)

            for hq in range(C_HEADS):
                kvh = hq // C_GROUP
                q = p_ref[rows, COL_CQ + hq * C_HD:COL_CQ + (hq + 1) * C_HD].astype(BF16)
                kb = p_ref[rows, COL_CK + kvh * C_HD:COL_CK + (kvh + 1) * C_HD].astype(BF16)
                vb = p_ref[rows, COL_CV + kvh * C_HD:COL_CV + (kvh + 1) * C_HD].astype(BF16)
                s = _dot_nt(q, kb) * scale
                snk = sink_ref[hq]
                m = jnp.maximum(jnp.max(s, axis=-1, keepdims=True), snk)
                e = jnp.exp(s - m)
                den = jnp.sum(e, axis=-1, keepdims=True) + jnp.exp(snk - m)
                o = _dot((e / den).astype(BF16), vb)
                oc_ref[rows, hq * C_HD:(hq + 1) * C_HD] = o.astype(BF16)

    full = lambda shape: pl.BlockSpec(shape, lambda b: (0,) * len(shape))
    out = jax.ShapeDtypeStruct((t, GROUP_WIDTH), BF16)
    ospec = pl.BlockSpec((bps * n, GROUP_WIDTH), lambda b: (b, 0))
    return pl.pallas_call(
        body,
        out_shape=(out, out, out),
        grid=(nbatch // bps,),
        in_specs=[pl.BlockSpec((bps * n, proj.shape[1]), lambda b: (b, 0)),
                  full(lam_l.shape), full(subln_l.shape),
                  pl.BlockSpec(memory_space=pltpu.SMEM),
                  full(w2.shape), full(csh.shape), full(csl.shape), full(wf.shape)],
        out_specs=(ospec, ospec, ospec),
        compiler_params=_cparams("parallel"),
        name="prompt_mixers",
    )(proj, lam_l, subln_l, sink_l, w2, csh, csl, wf)


def _sample_diff_attn(proj, cache_k, cache_v, layer, cos_t, sin_t, lam_l, subln_l,
                      lam_init, tp, nbatch, ns, tq):
    past = cache_k.shape[2]
    nk = ns + past
    qb0 = tp // tq
    qpb = ns // tq
    kb0 = tp // ns

    def body(q_ref, k_ref, v_ref, ck_ref, cv_ref, cq_ref, sq_ref, cosk_ref, sink_ref,
             lam_ref, g_ref, o_ref, kr_ref, vb_ref):
        @pl.when(pl.program_id(2) == 0)
        def _():
            kr_ref[0:ns, :] = _rope(k_ref[...], cosk_ref[...], sink_ref[...], A_QK // 2).astype(BF16)
            kr_ref[ns:nk, :] = ck_ref[...].astype(BF16)
            vb_ref[0:ns, :] = v_ref[...].astype(BF16)
            vb_ref[ns:nk, :] = cv_ref[...].astype(BF16)

        lam = _lam_value(lam_ref, lam_init)
        q = _rope(q_ref[...], cq_ref[...], sq_ref[...], A_QK // 2)
        o = _diff_attend_chunked(q, kr_ref, vb_ref, lam, KEY_CHUNK)
        o_ref[...] = _sub_ln(o, g_ref[...], lam_init).astype(BF16)

    assert nk % KEY_CHUNK == 0
    cache_spec = pl.BlockSpec((None, None, past, A_V), lambda b, h, i: (b, layer, 0, h))
    full = lambda shape: pl.BlockSpec(shape, lambda b, h, i: (0,) * len(shape))
    return pl.pallas_call(
        body,
        out_shape=jax.ShapeDtypeStruct((nbatch * ns, GROUP_WIDTH), BF16),
        grid=(nbatch, A_HEADS, qpb),
        in_specs=[pl.BlockSpec((tq, A_V), lambda b, h, i: (qb0 + b * qpb + i, COL_AQ // A_V + h)),
                  pl.BlockSpec((ns, A_V), lambda b, h, i: (kb0 + b, COL_AK // A_V + h)),
                  pl.BlockSpec((ns, A_V), lambda b, h, i: (kb0 + b, COL_AV // A_V + h)),
                  cache_spec, cache_spec,
                  pl.BlockSpec((tq, LANES), lambda b, h, i: (i, 0)),
                  pl.BlockSpec((tq, LANES), lambda b, h, i: (i, 0)),
                  full((ns, LANES)), full((ns, LANES)),
                  full(lam_l.shape), full(subln_l.shape)],
        out_specs=pl.BlockSpec((tq, A_V), lambda b, h, i: (b * qpb + i, h)),
        scratch_shapes=[pltpu.VMEM((nk, A_V), BF16), pltpu.VMEM((nk, A_V), BF16)],
        compiler_params=_cparams("parallel", "parallel", "arbitrary"),
        name="sample_diff_attn",
    )(proj, proj, proj, cache_k, cache_v, cos_t, sin_t, cos_t, sin_t, lam_l, subln_l)


def _sample_window_attn(proj, cache_k, cache_v, layer, cos_t, sin_t, sink_l,
                        tp, nbatch, ns):
    w = C_WINDOW
    past = cache_k.shape[2]
    nblk = ns // w
    qb0 = tp // w
    kb0 = tp // ns
    gw = C_GROUP * C_HD
    scale = C_HD ** -0.5

    def body(q_ref, k_ref, v_ref, ck_ref, cv_ref, cq_ref, sq_ref, cosk_ref, sink_ref,
             snk_ref, o_ref, kr_ref, vb_ref):
        kvh = pl.program_id(1)
        i = pl.program_id(2)

        @pl.when(i == 0)
        def _():
            kr_ref[...] = _rope(k_ref[...], cosk_ref[...], sink_ref[...], C_HD // 2).astype(BF16)
            vb_ref[...] = v_ref[...].astype(BF16)

        qs = [_rope(q_ref[:, g * C_HD:(g + 1) * C_HD], cq_ref[...], sq_ref[...], C_HD // 2)
              for g in range(C_GROUP)]
        qq = jnp.concatenate(qs, axis=0).astype(BF16)
        start = pl.multiple_of(jnp.clip((i - 1) * w, 0, ns - 3 * w), w)
        kw = kr_ref[pl.ds(start, 3 * w), :]
        vw = vb_ref[pl.ds(start, 3 * w), :]
        s_loc = _dot_nt(qq, kw) * scale
        rows = lax.broadcasted_iota(jnp.int32, s_loc.shape, 0)
        cols = lax.broadcasted_iota(jnp.int32, s_loc.shape, 1)
        qpos = i * w + (rows & (w - 1))
        kpos = start + cols
        s_loc = jnp.where(jnp.abs(kpos - qpos) <= w, s_loc, NEG)
        s_ctx = _dot_nt(qq, ck_ref[...].astype(BF16)) * scale
        rcol = lax.broadcasted_iota(jnp.int32, (C_GROUP * w, 1), 0)
        snk = jnp.zeros((C_GROUP * w, 1), F32)
        for g in range(C_GROUP):
            snk = jnp.where(rcol >= g * w, snk_ref[kvh * C_GROUP + g], snk)
        m = jnp.maximum(jnp.maximum(jnp.max(s_loc, axis=-1, keepdims=True),
                                    jnp.max(s_ctx, axis=-1, keepdims=True)), snk)
        e_loc = jnp.exp(s_loc - m)
        e_ctx = jnp.exp(s_ctx - m)
        den = (jnp.sum(e_loc, axis=-1, keepdims=True) + jnp.sum(e_ctx, axis=-1, keepdims=True)
               + jnp.exp(snk - m))
        o = (_dot((e_loc / den).astype(BF16), vw)
             + _dot((e_ctx / den).astype(BF16), cv_ref[...].astype(BF16)))
        for g in range(C_GROUP):
            o_ref[:, g * C_HD:(g + 1) * C_HD] = o[g * w:(g + 1) * w, :].astype(BF16)

    cache_spec = pl.BlockSpec((None, None, past, C_HD), lambda b, h, i: (b, layer, 0, h))
    full = lambda shape: pl.BlockSpec(shape, lambda b, h, i: (0,) * len(shape))
    return pl.pallas_call(
        body,
        out_shape=jax.ShapeDtypeStruct((nbatch * ns, GROUP_WIDTH), BF16),
        grid=(nbatch, C_KV_HEADS, nblk),
        in_specs=[pl.BlockSpec((w, gw), lambda b, h, i: (qb0 + b * nblk + i, COL_CQ // gw + h)),
                  pl.BlockSpec((ns, C_HD), lambda b, h, i: (kb0 + b, COL_CK // C_HD + h)),
                  pl.BlockSpec((ns, C_HD), lambda b, h, i: (kb0 + b, COL_CV // C_HD + h)),
                  cache_spec, cache_spec,
                  pl.BlockSpec((w, LANES), lambda b, h, i: (i, 0)),
                  pl.BlockSpec((w, LANES), lambda b, h, i: (i, 0)),
                  full((ns, LANES)), full((ns, LANES)),
                  pl.BlockSpec(memory_space=pltpu.SMEM)],
        out_specs=pl.BlockSpec((w, gw), lambda b, h, i: (b * nblk + i, h)),
        scratch_shapes=[pltpu.VMEM((ns, C_HD), BF16), pltpu.VMEM((ns, C_HD), BF16)],
        compiler_params=_cparams("parallel", "parallel", "arbitrary"),
        name="sample_window_attn",
    )(proj, proj, proj, cache_k, cache_v, cos_t, sin_t, cos_t, sin_t, sink_l)


def _sample_fourier(proj, w2, csh, csl, wf, tp, nbatch, ns, tr):
    rpb = ns // tr
    kb0 = tp // ns

    def body(u_ref, w2_ref, csh_ref, csl_ref, wf_ref, o_ref, t_ref):
        @pl.when(pl.program_id(1) == 0)
        def _():
            for g in range(B_GROUPS):
                sl = slice(g * B_GD, (g + 1) * B_GD)
                t_ref[:, sl] = _fourier_stage1(u_ref[:, sl], csh_ref[...], csl_ref[...]).astype(BF16)

        f = _dot(w2_ref[...], t_ref[...])
        o_ref[...] = _dot(f.astype(BF16), wf_ref[...]).astype(BF16)

    full = lambda shape: pl.BlockSpec(shape, lambda b, r: (0,) * len(shape))
    return pl.pallas_call(
        body,
        out_shape=jax.ShapeDtypeStruct((nbatch * ns, GROUP_WIDTH), BF16),
        grid=(nbatch, rpb),
        in_specs=[pl.BlockSpec((ns, GROUP_WIDTH), lambda b, r: (kb0 + b, COL_BU // GROUP_WIDTH)),
                  pl.BlockSpec((tr, 2 * ns), lambda b, r: (r, 0)),
                  full(csh.shape), full(csl.shape), full(wf.shape)],
        out_specs=pl.BlockSpec((tr, GROUP_WIDTH), lambda b, r: (b * rpb + r, 0)),
        scratch_shapes=[pltpu.VMEM((2 * ns, GROUP_WIDTH), BF16)],
        compiler_params=_cparams("parallel", "arbitrary"),
        name="sample_fourier",
    )(proj, w2, csh, csl, wf)


def _spatial_gate(proj, g, b, ws, bst, tr):
    t = proj.shape[0]

    def body(u_ref, v_ref, g_ref, b_ref, ws_ref, bs_ref, o_ref):
        vn = (_ln_rows(v_ref[...]) * g_ref[...] + b_ref[...]).astype(BF16)
        for c in range(tr // CHUNK):
            rs = slice(c * CHUNK, (c + 1) * CHUNK)
            for h in range(D_HEADS):
                cs = slice(h * D_HD, (h + 1) * D_HD)
                mixed = _dot(ws_ref[h], vn[rs, cs]) + bs_ref[:, h:h + 1]
                o_ref[rs, cs] = (u_ref[rs, cs] * mixed).astype(BF16)

    full = lambda shape: pl.BlockSpec(shape, lambda i: (0,) * len(shape))
    return pl.pallas_call(
        body,
        out_shape=jax.ShapeDtypeStruct((t, GROUP_WIDTH), BF16),
        grid=(t // tr,),
        in_specs=[pl.BlockSpec((tr, GROUP_WIDTH), lambda i: (i, COL_DU // GROUP_WIDTH)),
                  pl.BlockSpec((tr, GROUP_WIDTH), lambda i: (i, COL_DV // GROUP_WIDTH)),
                  full(g.shape), full(b.shape), full(ws.shape), full(bst.shape)],
        out_specs=pl.BlockSpec((tr, GROUP_WIDTH), lambda i: (i, 0)),
        compiler_params=_cparams("parallel"),
        name="spatial_gate",
    )(proj, proj, g, b, ws, bst)


def _post_ln(x, f, gate, g, b):
    return _ln_rows(ALPHA * x + gate * f) * g + b


OUT_SUB_ROWS = 256
ROUTER_TERM_LANES = 8


def _pack_router(w_router_l, b_router_l):
    d = w_router_l.shape[0]
    assert N_EXPERTS <= ROUTER_TERM_LANES
    wp = jnp.zeros((d, LANES), BF16)
    for k, term in enumerate(_split3(w_router_l)):
        wp = wp.at[:, k * ROUTER_TERM_LANES:k * ROUTER_TERM_LANES + N_EXPERTS].set(term)
    br = jnp.zeros((1, LANES), F32).at[0, :N_EXPERTS].set(b_router_l)
    return wp, br


def _out_proj(abc_p, abc_s, od, w_all, layer, x, mods_l, g1, b1, router, tp, ns, tm):
    t, d = x.shape
    with_router = router is not None
    np_tiles = tp // tm
    sub = min(OUT_SUB_ROWS, tm)

    def body(ap_ref, bp_ref, cp_ref, as_ref, bs_ref, cs_ref, od_ref, w_ref, x_ref, m_ref,
             g_ref, b_ref, *rest):
        if with_router:
            wr_ref, br_ref, x1_ref, h2_ref, lg_ref, cat_ref = rest
        else:
            x1_ref, h2_ref, cat_ref = rest
        gw = GROUP_WIDTH
        is_ctx = pl.program_id(0) < np_tiles
        cat_ref[:, 0:gw] = jnp.where(is_ctx, ap_ref[...], as_ref[...])
        cat_ref[:, gw:2 * gw] = jnp.where(is_ctx, bp_ref[...], bs_ref[...])
        cat_ref[:, 2 * gw:3 * gw] = jnp.where(is_ctx, cp_ref[...], cs_ref[...])
        cat_ref[:, 3 * gw:4 * gw] = od_ref[...]
        for r in range(tm // sub):
            rs = slice(r * sub, (r + 1) * sub)
            mix = _dot(cat_ref[rs, :], w_ref[...])
            x1 = _post_ln(x_ref[rs, :], mix, m_ref[0, 2:3, :], g_ref[...], b_ref[...])
            x1_ref[rs, :] = x1
            h2 = _mod_ln(x1, m_ref, 3)
            h2_ref[rs, :] = h2.astype(h2_ref.dtype)
            if with_router:
                h0, h1, h2b = _split3(h2)
                p0, p1, p2 = (_dot(hk, wr_ref[...]) for hk in (h0, h1, h2b))
                left = lambda p, k: pltpu.roll(p, LANES - k * ROUTER_TERM_LANES, axis=1)
                lg = p0 + (left(p0, 1) + p1) + (left(p0, 2) + left(p1, 1) + p2)
                lg_ref[rs, :] = lg + br_ref[...]

    row = functools.partial(_mod_row, tm=tm, tp=tp, ns=ns)
    full = lambda shape: pl.BlockSpec(shape, lambda i: (0,) * len(shape))
    tile = lambda n: pl.BlockSpec((tm, n), lambda i: (i, 0))
    ctx_tile = pl.BlockSpec((tm, GROUP_WIDTH), lambda i: (jnp.minimum(i, np_tiles - 1), 0))
    lat_tile = pl.BlockSpec((tm, GROUP_WIDTH), lambda i: (jnp.maximum(i - np_tiles, 0), 0))
    w_spec = pl.BlockSpec((None,) + w_all.shape[1:], lambda i: (layer, 0, 0),
                          pipeline_mode=pl.Buffered(1))
    in_specs = [ctx_tile] * 3 + [lat_tile] * 3 + [
        tile(GROUP_WIDTH), w_spec, tile(d),
        pl.BlockSpec((1, 6, d), lambda i: (row(i), 0, 0)), full(g1.shape), full(b1.shape)]
    args = [*abc_p, *abc_s, od, w_all, x, mods_l, g1, b1]
    out_shape = [jax.ShapeDtypeStruct((t, d), F32),
                 jax.ShapeDtypeStruct((t, d), F32 if with_router else BF16)]
    out_specs = [tile(d), tile(d)]
    if with_router:
        in_specs += [full(r.shape) for r in router]
        args += list(router)
        out_shape.append(jax.ShapeDtypeStruct((t, LANES), F32))
        out_specs.append(tile(LANES))
    return pl.pallas_call(
        body, out_shape=tuple(out_shape), grid=(t // tm,),
        in_specs=in_specs, out_specs=tuple(out_specs),
        scratch_shapes=[pltpu.VMEM((tm, 4 * GROUP_WIDTH), BF16)],
        compiler_params=_cparams("parallel"),
        name="out_proj",
    )(*args)


def _ffn_up(h2, wg, wu, layer, moe_w, moe_layer, tm, tn):
    t, d = h2.shape
    dff = wg.shape[2]
    nj, ni = dff // tn, t // tm
    _, n_exp, _, dffe = moe_w[0].shape
    rows, cols = d // ni, dffe // nj
    assert rows * ni == d and cols * nj == dffe and rows % 16 == 0 and cols % LANES == 0

    def body(h_ref, wg_ref, wu_ref, mg_ref, mu_ref, md_ref, a_ref, og_ref, ou_ref, od_ref):
        h = h_ref[...]
        a_ref[...] = (_silu(_dot(h, wg_ref[...])) * _dot(h, wu_ref[...])).astype(BF16)
        for src, dst in ((mg_ref, og_ref), (mu_ref, ou_ref), (md_ref, od_ref)):
            dst[...] = src[...].astype(BF16)

    up_in = pl.BlockSpec((None, n_exp, rows, cols), lambda j, i: (moe_layer, 0, i, j))
    up_out = pl.BlockSpec((n_exp, rows, cols), lambda j, i: (0, i, j))
    down_in = pl.BlockSpec((None, n_exp, cols, rows), lambda j, i: (moe_layer, 0, j, i))
    down_out = pl.BlockSpec((n_exp, cols, rows), lambda j, i: (0, j, i))
    return pl.pallas_call(
        body,
        out_shape=(jax.ShapeDtypeStruct((t, dff), BF16),
                   jax.ShapeDtypeStruct((n_exp, d, dffe), BF16),
                   jax.ShapeDtypeStruct((n_exp, d, dffe), BF16),
                   jax.ShapeDtypeStruct((n_exp, dffe, d), BF16)),
        grid=(nj, ni),
        in_specs=[pl.BlockSpec((tm, d), lambda j, i: (i, 0)),
                  pl.BlockSpec((None, d, tn), lambda j, i: (layer, 0, j)),
                  pl.BlockSpec((None, d, tn), lambda j, i: (layer, 0, j)),
                  up_in, up_in, down_in],
        out_specs=(pl.BlockSpec((tm, tn), lambda j, i: (i, j)), up_out, up_out, down_out),
        compiler_params=_cparams("parallel", "parallel"),
        name="ffn_up",
    )(h2, wg, wu, *moe_w)


def _ffn_down(a, wd, layer, x1, mods_l, mods_next, g2, b2, tp, ns, tm):
    t, d = x1.shape
    dff = a.shape[1]

    def body(a_ref, wd_ref, x_ref, m_ref, mn_ref, g_ref, b_ref, o_ref, hn_ref):
        f = _dot(a_ref[...], wd_ref[...])
        x2 = _post_ln(x_ref[...], f, m_ref[0, 5:6, :], g_ref[...], b_ref[...])
        o_ref[...] = x2
        hn_ref[...] = _mod_ln(x2, mn_ref, 0).astype(BF16)

    row = functools.partial(_mod_row, tm=tm, tp=tp, ns=ns)
    full = lambda shape: pl.BlockSpec(shape, lambda i: (0,) * len(shape))
    mod_spec = pl.BlockSpec((1, 6, d), lambda i: (row(i), 0, 0))
    tile = pl.BlockSpec((tm, d), lambda i: (i, 0))
    return pl.pallas_call(
        body,
        out_shape=(jax.ShapeDtypeStruct((t, d), F32), jax.ShapeDtypeStruct((t, d), BF16)),
        grid=(t // tm,),
        in_specs=[pl.BlockSpec((tm, dff), lambda i: (i, 0)),
                  pl.BlockSpec((None, dff, d), lambda i: (layer, 0, 0), pipeline_mode=pl.Buffered(1)),
                  tile, mod_spec, mod_spec, full(g2.shape), full(b2.shape)],
        out_specs=(tile, tile),
        compiler_params=_cparams("parallel"),
        name="ffn_down",
    )(a, wd, x1, mods_l, mods_next, g2, b2)


ROUTE_I1, ROUTE_I2, ROUTE_W1, ROUTE_W2, ROUTE_R1, ROUTE_R2 = range(6)


def _route(logits, tr):
    t = logits.shape[0]

    def body(lg_ref, route_ref, cnt_ref, carry_ref):
        @pl.when(pl.program_id(0) == 0)
        def _():
            carry_ref[...] = jnp.zeros_like(carry_ref)

        lane = lax.broadcasted_iota(jnp.int32, (tr, LANES), 1)
        lane_f = lane.astype(F32)
        lg = jnp.where(lane < N_EXPERTS, lg_ref[...], -jnp.inf)
        m1 = jnp.max(lg, axis=-1, keepdims=True)
        i1 = jnp.min(jnp.where(lg == m1, lane_f, float(LANES)), axis=-1, keepdims=True)
        lg2 = jnp.where(lane_f == i1, -jnp.inf, lg)
        m2 = jnp.max(lg2, axis=-1, keepdims=True)
        i2 = jnp.min(jnp.where(lg2 == m2, lane_f, float(LANES)), axis=-1, keepdims=True)
        e2 = jnp.exp(m2 - m1)
        den = 1.0 + e2
        oh1 = lane_f == i1
        oh2 = lane_f == i2
        sel = jnp.where(oh1 | oh2, 1.0, 0.0)
        ri = lax.broadcasted_iota(jnp.int32, (tr, tr), 0)
        ci = lax.broadcasted_iota(jnp.int32, (tr, tr), 1)
        tri = jnp.where(ri > ci, 1.0, 0.0).astype(BF16)
        prefix = _dot(tri, sel.astype(BF16)) + carry_ref[0:1, :]
        r1 = jnp.sum(jnp.where(oh1, prefix, 0.0), axis=-1, keepdims=True)
        r2 = jnp.sum(jnp.where(oh2, prefix, 0.0), axis=-1, keepdims=True)
        carry_ref[0:1, :] = carry_ref[0:1, :] + jnp.sum(sel, axis=0, keepdims=True)
        out = jnp.zeros((tr, LANES), F32)
        for idx, val in ((ROUTE_I1, i1), (ROUTE_I2, i2), (ROUTE_W1, 1.0 / den),
                         (ROUTE_W2, e2 / den), (ROUTE_R1, r1), (ROUTE_R2, r2)):
            out = jnp.where(lane == idx, val, out)
        route_ref[...] = out
        cnt_ref[...] = carry_ref[...]

    return pl.pallas_call(
        body,
        out_shape=(jax.ShapeDtypeStruct((t, LANES), F32), jax.ShapeDtypeStruct((8, LANES), F32)),
        grid=(t // tr,),
        in_specs=[pl.BlockSpec((tr, LANES), lambda i: (i, 0))],
        out_specs=(pl.BlockSpec((tr, LANES), lambda i: (i, 0)),
                   pl.BlockSpec((8, LANES), lambda i: (0, 0))),
        scratch_shapes=[pltpu.VMEM((8, LANES), F32)],
        compiler_params=_cparams("arbitrary"),
        name="route",
    )(logits)


ROW_COPY_UNROLL = 8


def _row_sources(pos1, pos2, fill_from, fill_to, n_rows):
    t = pos1.shape[0]

    def body(p1_ref, p2_ref, lo_ref, hi_ref, src_ref):
        def clear(r, carry):
            src_ref[r] = 0
            return carry

        def put(tok, carry):
            src_ref[p1_ref[tok]] = tok
            src_ref[p2_ref[tok]] = tok
            return carry

        lax.fori_loop(0, t, put, 0, unroll=ROW_COPY_UNROLL)
        for e in range(N_EXPERTS):
            lax.fori_loop(lo_ref[e], hi_ref[e], clear, 0)
        lax.fori_loop(hi_ref[N_EXPERTS - 1], n_rows, clear, 0)

    smem = pl.BlockSpec(memory_space=pltpu.SMEM)
    return pl.pallas_call(
        body,
        out_shape=jax.ShapeDtypeStruct((n_rows,), jnp.int32),
        in_specs=[smem] * 4, out_specs=smem,
        name="row_sources",
    )(pos1, pos2, fill_from, fill_to)


def _experts(tile_expert, n_used, src, h, wg, wu, wd, tm, ck):
    d = h.shape[1]
    nt = src.shape[0] // tm
    dff = wg.shape[2]

    def row_copy(src_ref, h_ref, xg_ref, sem, tile, slot, r):
        return pltpu.make_async_copy(h_ref.at[pl.ds(src_ref[tile * tm + r], 1)],
                                     xg_ref.at[slot, pl.ds(r, 1)], sem.at[slot])

    def body(te_ref, nu_ref, src_ref, h_ref, wg_ref, wu_ref, wd_ref, y_ref, xg_ref, a_ref, sem):
        del te_ref
        i = pl.program_id(0)
        nu = nu_ref[0]

        def issue(tile, slot):
            def step(r, carry):
                row_copy(src_ref, h_ref, xg_ref, sem, tile, slot, r).start()
                return carry
            lax.fori_loop(0, tm, step, 0, unroll=ROW_COPY_UNROLL)

        def drain(tile, slot):
            del tile
            pltpu.make_async_copy(h_ref.at[pl.ds(0, tm)], xg_ref.at[slot], sem.at[slot]).wait()

        @pl.when(i == 0)
        def _():
            issue(0, 0)

        @pl.when(i < nu)
        def _():
            slot = i & 1
            drain(i, slot)
            x = xg_ref[slot].astype(BF16)
            nxt = jnp.minimum(i + 1, nu - 1)
            nc = dff // ck
            for c in range(nc):
                cs = slice(c * ck, (c + 1) * ck)
                a_ref[:, cs] = (_silu(_dot(x, wg_ref[:, cs])) * _dot(x, wu_ref[:, cs])).astype(BF16)
                for r in range(c * tm // nc, (c + 1) * tm // nc):
                    row_copy(src_ref, h_ref, xg_ref, sem, nxt, 1 - slot, r).start()
            y_ref[...] = _dot(a_ref[...], wd_ref[...])

            @pl.when(i == nu - 1)
            def _():
                drain(nxt, 1 - slot)

        @pl.when(i >= nu)
        def _():
            y_ref[...] = jnp.zeros_like(y_ref)

    def expert_weights(shape):
        return pl.BlockSpec((None,) + shape, lambda i, te, nu, src: (te[i], 0, 0),
                            pipeline_mode=pl.Buffered(1))

    return pl.pallas_call(
        body,
        out_shape=jax.ShapeDtypeStruct((nt * tm, d), F32),
        grid_spec=pltpu.PrefetchScalarGridSpec(
            num_scalar_prefetch=3, grid=(nt,),
            in_specs=[pl.BlockSpec(memory_space=pl.ANY),
                      expert_weights((d, dff)), expert_weights((d, dff)),
                      expert_weights((dff, d))],
            out_specs=pl.BlockSpec((tm, d), lambda i, te, nu, src: (i, 0)),
            scratch_shapes=[pltpu.VMEM((2, tm, d), F32), pltpu.VMEM((tm, dff), BF16),
                            pltpu.SemaphoreType.DMA((2,))]),
        compiler_params=_cparams("arbitrary"),
        name="experts",
    )(tile_expert, n_used, src, h, wg, wu, wd)


COMBINE_ROWS = 32


def _combine(pos1, pos2, ys, route, x1, mods_l, mods_next, g2, b2, tp, ns, tc):
    t, d = x1.shape
    is_last = mods_next is None
    np_tiles = tp // tc

    def row_copies(p1_ref, p2_ref, ys_ref, y1_ref, y2_ref, sem, tile, slot, r):
        tok = tile * tc + r
        return (pltpu.make_async_copy(ys_ref.at[pl.ds(p1_ref[tok], 1)],
                                      y1_ref.at[slot, pl.ds(r, 1)], sem.at[0, slot]),
                pltpu.make_async_copy(ys_ref.at[pl.ds(p2_ref[tok], 1)],
                                      y2_ref.at[slot, pl.ds(r, 1)], sem.at[1, slot]))

    def body(p1_ref, p2_ref, ys_ref, rt_ref, x_ref, m_ref, *rest):
        if is_last:
            g_ref, b_ref, yp_ref, yl_ref, y1_ref, y2_ref, sem, o_ref = rest
        else:
            mn_ref, g_ref, b_ref, o_ref, hn_ref, y1_ref, y2_ref, sem = rest
        j = pl.program_id(0)

        def issue(tile, slot):
            def step(r, carry):
                for cp in row_copies(p1_ref, p2_ref, ys_ref, y1_ref, y2_ref, sem, tile, slot, r):
                    cp.start()
                return carry
            lax.fori_loop(0, tc, step, 0, unroll=ROW_COPY_UNROLL)

        def drain(slot):
            for k, buf in enumerate((y1_ref, y2_ref)):
                pltpu.make_async_copy(ys_ref.at[pl.ds(0, tc)], buf.at[slot], sem.at[k, slot]).wait()

        @pl.when(j == 0)
        def _():
            issue(0, 0)

        slot = j & 1
        drain(slot)
        last = pl.num_programs(0) - 1
        nxt = jnp.minimum(j + 1, last)
        for blk in range(tc // COMBINE_ROWS):
            rs = slice(blk * COMBINE_ROWS, (blk + 1) * COMBINE_ROWS)
            rt = rt_ref[rs, :]
            f = (rt[:, ROUTE_W1:ROUTE_W1 + 1] * y1_ref[slot, rs, :]
                 + rt[:, ROUTE_W2:ROUTE_W2 + 1] * y2_ref[slot, rs, :])
            x2 = _post_ln(x_ref[rs, :], f, m_ref[0, 5:6, :], g_ref[...], b_ref[...])
            o_ref[rs, :] = x2
            if not is_last:
                hn_ref[rs, :] = _mod_ln(x2, mn_ref, 0).astype(BF16)
            for r in range(blk * COMBINE_ROWS, (blk + 1) * COMBINE_ROWS):
                for cp in row_copies(p1_ref, p2_ref, ys_ref, y1_ref, y2_ref, sem, nxt, 1 - slot, r):
                    cp.start()

        @pl.when(j == last)
        def _():
            drain(1 - slot)

        if is_last:
            @pl.when(j < np_tiles)
            def _():
                yp_ref[...] = o_ref[...]

            @pl.when(j >= np_tiles)
            def _():
                yl_ref[...] = o_ref[...]

    row = functools.partial(_mod_row, tm=tc, tp=tp, ns=ns)
    full = lambda shape: pl.BlockSpec(shape, lambda i, p1, p2: (0,) * len(shape))
    tile = pl.BlockSpec((tc, d), lambda i, p1, p2: (i, 0))
    mod_spec = pl.BlockSpec((1, 6, d), lambda i, p1, p2: (row(i), 0, 0))
    in_specs = [pl.BlockSpec(memory_space=pl.ANY),
                pl.BlockSpec((tc, LANES), lambda i, p1, p2: (i, 0)), tile, mod_spec]
    args = [pos1, pos2, ys, route, x1, mods_l]
    scratch = [pltpu.VMEM((2, tc, d), F32), pltpu.VMEM((2, tc, d), F32),
               pltpu.SemaphoreType.DMA((2, 2))]
    if is_last:
        out_shape = (jax.ShapeDtypeStruct((tp, d), F32), jax.ShapeDtypeStruct((t - tp, d), F32))
        out_specs = (pl.BlockSpec((tc, d), lambda i, p1, p2: (jnp.minimum(i, np_tiles - 1), 0)),
                     pl.BlockSpec((tc, d), lambda i, p1, p2: (jnp.maximum(i - np_tiles, 0), 0)))
        scratch.append(pltpu.VMEM((tc, d), F32))
    else:
        in_specs.append(mod_spec)
        args.append(mods_next)
        out_shape = (jax.ShapeDtypeStruct((t, d), F32), jax.ShapeDtypeStruct((t, d), BF16))
        out_specs = (tile, tile)
    in_specs += [full(g2.shape), full(b2.shape)]
    args += [g2, b2]
    return pl.pallas_call(
        body,
        out_shape=out_shape,
        grid_spec=pltpu.PrefetchScalarGridSpec(
            num_scalar_prefetch=2, grid=(t // tc,),
            in_specs=in_specs, out_specs=out_specs, scratch_shapes=scratch),
        compiler_params=_cparams("arbitrary"),
        name="combine",
    )(*args)


def _ffn_routed(h2, logits, wg, wu, wd, x1, mods_l, mods_next, g2, b2, tp, ns, tm, ck):
    t = h2.shape[0]
    route, cnt = _route(logits, min(512, t))
    i1 = route[:, ROUTE_I1].astype(jnp.int32)
    i2 = route[:, ROUTE_I2].astype(jnp.int32)
    counts = cnt[0, :N_EXPERTS].astype(jnp.int32)
    padded = (counts + tm - 1) // tm * tm
    ends = jnp.cumsum(padded)
    starts = ends - padded
    pos1 = starts[i1] + route[:, ROUTE_R1].astype(jnp.int32)
    pos2 = starts[i2] + route[:, ROUTE_R2].astype(jnp.int32)
    nt = 2 * t // tm + N_EXPERTS
    n_used = ends[-1] // tm
    tile_start = jnp.arange(nt, dtype=jnp.int32) * tm
    tile_expert = jnp.sum(tile_start[:, None] >= ends[None, :], axis=1).astype(jnp.int32)
    tile_expert = jnp.minimum(tile_expert, tile_expert[n_used - 1])
    src = _row_sources(pos1, pos2, starts + counts, ends, nt * tm)
    ys = _experts(tile_expert, n_used.reshape(1), src, h2, wg, wu, wd, tm, ck)
    return _combine(pos1, pos2, ys, route, x1, mods_l, mods_next, g2, b2, tp, ns, tm)


def _permute_in_cols(w):
    d = w.shape[1]
    qk = w[..., :4 * A_HEADS * A_QK].reshape(w.shape[0], d, 2, 2, A_HEADS, A_QK)
    qk = qk.transpose(0, 1, 2, 4, 3, 5).reshape(w.shape[0], d, 4 * A_HEADS * A_QK)
    return jnp.concatenate([qk, w[..., 4 * A_HEADS * A_QK:]], axis=-1)


def kernel(x_prompt, x_sample, cache_a_k, cache_a_v, cache_c_k, cache_c_v, c, c_ctx, w_ada, b_ada, w_in, lam, a_subln, w_fnet, c_sink, d_norm_g, d_norm_b, d_ws, d_bs, w_out, ln1_g, ln1_b, ln2_g, ln2_b, ffn_wg, ffn_wu, ffn_wd, w_router, b_router, moe_wg, moe_wu, moe_wd):
    nb_p, n_p, d = x_prompt.shape
    nb_s, n_s, _ = x_sample.shape
    depth = w_in.shape[0]
    past = cache_a_k.shape[2]
    tp = nb_p * n_p
    t = tp + nb_s * n_s
    assert tp % n_s == 0 and n_s % 512 == 0 and n_p % CHUNK == 0 and nb_s + 1 <= 8

    assert depth % 2 == 0
    c_all = jnp.zeros((8, d), F32).at[0].set(c_ctx).at[1:1 + nb_s].set(c)
    mods = _modulation(c_all, w_ada, b_ada).reshape(depth, 8, 6, d)

    w_in_b = _permute_in_cols(w_in.astype(BF16))
    w_out_b = w_out.astype(BF16)
    w_fnet_b = w_fnet.astype(BF16)
    d_ws_b = d_ws.astype(BF16)
    ffn_wg_b, ffn_wu_b, ffn_wd_b = (w.astype(BF16) for w in (ffn_wg, ffn_wu, ffn_wd))
    rope_a = _rope_tables(n_s, A_QK)
    rope_c = _rope_tables(n_s, C_HD)
    csh, csl = _split(jnp.asarray(_dft_channel_table()))
    w2p_h = jnp.asarray(_dft_table(n_p)).astype(BF16)
    w2s_h = jnp.asarray(_dft_table(n_s)).astype(BF16)
    ca_k = cache_a_k.reshape(nb_s, depth, past, A_HEADS * A_V)
    ca_v = cache_a_v.reshape(nb_s, depth, past, A_HEADS * A_V)
    cc_k = cache_c_k.reshape(nb_s, depth, past, C_KV_HEADS * C_HD)
    cc_v = cache_c_v.reshape(nb_s, depth, past, C_KV_HEADS * C_HD)

    tm_big = math.gcd(1024, tp, n_s)
    tm_mid = math.gcd(512, tp, n_s)
    tm_small = math.gcd(256, tp, n_s)
    caches = tuple(jnp.zeros((nb_p, depth, n_p, cut[2]), F32) for cut in CACHE_CUTS)
    x, h = _first_ln(x_prompt.reshape(tp, d), x_sample.reshape(nb_s * n_s, d), mods[0], n_s, tm_mid)
    for l in range(depth):
        lam_init = 0.8 - 0.6 * math.exp(-0.3 * l)
        mods_next = mods[l + 1] if l + 1 < depth else None
        proj, *caches = _in_proj(h, w_in_b, l, caches, tp, tm_big)

        subln = a_subln[l].reshape(1, A_V)
        abc_p = _prompt_mixers(proj, lam[l], subln, c_sink[l], w2p_h, csh, csl,
                               w_fnet_b[l], lam_init, nb_p, n_p)
        oa_s = _sample_diff_attn(proj, ca_k, ca_v, l, rope_a[0], rope_a[1], lam[l], subln,
                                 lam_init, tp, nb_s, n_s, tm_big)
        oc_s = _sample_window_attn(proj, cc_k, cc_v, l, rope_c[0], rope_c[1], c_sink[l],
                                   tp, nb_s, n_s)
        ob_s = _sample_fourier(proj, w2s_h, csh, csl, w_fnet_b[l], tp, nb_s, n_s, tm_mid)
        abc_s = (oa_s, ob_s, oc_s)
        od = _spatial_gate(proj, d_norm_g[l].reshape(1, -1), d_norm_b[l].reshape(1, -1),
                           d_ws_b[l], d_bs[l].T, tm_mid)

        g1, b1 = ln1_g[l].reshape(1, d), ln1_b[l].reshape(1, d)
        g2, b2 = ln2_g[l].reshape(1, d), ln2_b[l].reshape(1, d)
        j = l // 2
        if l % 2 == 0:
            x1, h2 = _out_proj(abc_p, abc_s, od, w_out_b, l, x, mods[l], g1, b1, None,
                               tp, n_s, tm_mid)
            a, *moe_b = _ffn_up(h2, ffn_wg_b, ffn_wu_b, j, (moe_wg, moe_wu, moe_wd), j, tm_big, 512)
            x, h = _ffn_down(a, ffn_wd_b, j, x1, mods[l], mods_next, g2, b2, tp, n_s, tm_small)
        else:
            router = _pack_router(w_router[j], b_router[j])
            x1, h2, logits = _out_proj(abc_p, abc_s, od, w_out_b, l, x, mods[l], g1, b1, router,
                                       tp, n_s, tm_mid)
            x, h = _ffn_routed(h2, logits, *moe_b, x1, mods[l], mods_next,
                               g2, b2, tp, n_s, tm_small, 256)

    y_p = x.reshape(nb_p, n_p, d)
    y_s = h.reshape(nb_s, n_s, d)
    ak, av, ck, cv = caches
    return (y_p, y_s,
            ak.reshape(nb_p, depth, n_p, A_HEADS, 2 * A_QK), av.reshape(nb_p, depth, n_p, A_HEADS, A_V),
            ck.reshape(nb_p, depth, n_p, C_KV_HEADS, C_HD), cv.reshape(nb_p, depth, n_p, C_KV_HEADS, C_HD))
```

```python
import functools
import math

import jax
import jax.numpy as jnp
import numpy as np
from jax import lax
from jax.experimental import pallas as pl
from jax.experimental.pallas import tpu as pltpu

F32 = jnp.float32
BF16 = jnp.bfloat16

GRID_W = 64
A_V = 128
A_QK = 64
A_HEADS = 4
B_GD = 128
B_GROUPS = 4
C_HD = 128
C_HEADS = 4
C_KV_HEADS = 2
C_GROUP = C_HEADS // C_KV_HEADS
C_WINDOW = 128
D_HD = 128
D_HEADS = 4
CHUNK = 128
GROUP_WIDTH = 512
N_EXPERTS = 8
ROPE_BASE = 10000.0
LN_EPS = 1e-6
SUBLN_EPS = 1e-5
NEG = -1e30
TOTAL_DEPTH = 4
ALPHA = (2.0 * TOTAL_DEPTH) ** 0.25

COL_AQ = 0
COL_AK = 512
COL_AV = 1024
COL_BU = 1536
COL_CQ = 2048
COL_CK = 2560
COL_CV = 2816
COL_DU = 3072
COL_DV = 3584

LANES = 128
V7X_VMEM_BYTES = 64 * 1024 * 1024
VMEM_LIMIT = V7X_VMEM_BYTES - 8 * 1024 * 1024


def _cparams(*sem):
    return pltpu.CompilerParams(dimension_semantics=sem, vmem_limit_bytes=VMEM_LIMIT)


def _dot(a, b):
    return jnp.dot(a, b, preferred_element_type=F32)


def _dot_nt(a, b):
    return lax.dot_general(a, b, (((1,), (1,)), ((), ())), preferred_element_type=F32)


def _split(a):
    hi = a.astype(BF16)
    lo = (a - hi.astype(F32)).astype(BF16)
    return hi, lo


def _split3(a):
    a0 = a.astype(BF16)
    r1 = a - a0.astype(F32)
    a1 = r1.astype(BF16)
    a2 = (r1 - a1.astype(F32)).astype(BF16)
    return a0, a1, a2


def _dot3(a_hi, a_lo, b_hi, b_lo):
    return _dot(a_hi, b_hi) + (_dot(a_hi, b_lo) + _dot(a_lo, b_hi))


def _ln_rows(x):
    xc = x - jnp.mean(x, axis=-1, keepdims=True)
    return xc * lax.rsqrt(jnp.mean(xc * xc, axis=-1, keepdims=True) + LN_EPS)


def _silu(x):
    return x / (1.0 + jnp.exp(-x))


def _mod_row(i, tm, tp, ns):
    return jnp.where(i * tm < tp, 0, 1 + (i * tm - tp) // ns)


def _modulation(c_all, w_ada, b_ada):
    depth, d, n6 = w_ada.shape
    rows = c_all.shape[0]
    tn = 512

    def body(c_ref, w_ref, b_ref, o_ref):
        s = _silu(c_ref[...]).astype(BF16)
        o_ref[0] = _dot(s, w_ref[0].astype(BF16)) + b_ref[0]

    return pl.pallas_call(
        body,
        out_shape=jax.ShapeDtypeStruct((depth, rows, n6), F32),
        grid=(depth, n6 // tn),
        in_specs=[pl.BlockSpec((rows, d), lambda l, j: (0, 0)),
                  pl.BlockSpec((1, d, tn), lambda l, j: (l, 0, j)),
                  pl.BlockSpec((1, 1, tn), lambda l, j: (l, 0, j))],
        out_specs=pl.BlockSpec((1, rows, tn), lambda l, j: (l, 0, j)),
        compiler_params=_cparams("parallel", "parallel"),
        name="modulation",
    )(c_all, w_ada, b_ada.reshape(depth, 1, n6))


def _mod_ln(x, m_ref, k):
    return _ln_rows(x) * (1.0 + m_ref[0, k + 1:k + 2, :]) + m_ref[0, k:k + 1, :]


def _first_ln(x_p, x_s, mods_l, ns, tm):
    tp, d = x_p.shape
    t = tp + x_s.shape[0]
    np_tiles = tp // tm

    def body(xp_ref, xs_ref, m_ref, x_ref, h_ref):
        x = jnp.where(pl.program_id(0) < np_tiles, xp_ref[...], xs_ref[...])
        x_ref[...] = x
        h_ref[...] = _mod_ln(x, m_ref, 0).astype(BF16)

    row = functools.partial(_mod_row, tm=tm, tp=tp, ns=ns)
    tile = pl.BlockSpec((tm, d), lambda i: (i, 0))
    return pl.pallas_call(
        body,
        out_shape=(jax.ShapeDtypeStruct((t, d), F32), jax.ShapeDtypeStruct((t, d), BF16)),
        grid=(t // tm,),
        in_specs=[pl.BlockSpec((tm, d), lambda i: (jnp.minimum(i, np_tiles - 1), 0)),
                  pl.BlockSpec((tm, d), lambda i: (jnp.maximum(i - np_tiles, 0), 0)),
                  pl.BlockSpec((1, 6, d), lambda i: (row(i), 0, 0))],
        out_specs=(tile, tile),
        compiler_params=_cparams("parallel"),
        name="first_ln",
    )(x_p, x_s, mods_l)


IN_TN = 1024
CACHE_CUTS = ((COL_AK // IN_TN, COL_AK % IN_TN, COL_AV - COL_AK),
              (COL_AV // IN_TN, COL_AV % IN_TN, COL_BU - COL_AV),
              (COL_CK // IN_TN, COL_CK % IN_TN, COL_CV - COL_CK),
              (COL_CV // IN_TN, COL_CV % IN_TN, COL_DU - COL_CV))


def _in_proj(h, w_all, layer, caches, tp, tm):
    t, d = h.shape
    n = w_all.shape[2]
    tn = IN_TN
    np_tiles = tp // tm
    n_p = caches[0].shape[2]
    bpt = tm // n_p
    assert bpt * n_p == tm

    def body(h_ref, w_ref, *rest):
        o_ref = rest[len(caches)]
        cache_refs = rest[len(caches) + 1:]
        i = pl.program_id(0)
        j = pl.program_id(1)
        o_ref[...] = _dot(h_ref[...], w_ref[...])

        for c_ref, (jt, c0, width) in zip(cache_refs, CACHE_CUTS):
            @pl.when(jnp.logical_and(j == jt, i < np_tiles))
            def _(c_ref=c_ref, c0=c0, width=width):
                for bb in range(bpt):
                    c_ref[bb] = o_ref[bb * n_p:(bb + 1) * n_p, c0:c0 + width]

    def cache_spec(c):
        return pl.BlockSpec((bpt, None, n_p, c.shape[3]),
                            lambda i, j: (jnp.minimum(i, np_tiles - 1), layer, 0, 0))

    n_in = 2
    return pl.pallas_call(
        body,
        out_shape=(jax.ShapeDtypeStruct((t, n), F32),
                   *[jax.ShapeDtypeStruct(c.shape, c.dtype) for c in caches]),
        grid=(t // tm, n // tn),
        in_specs=[pl.BlockSpec((tm, d), lambda i, j: (i, 0)),
                  pl.BlockSpec((None, d, tn), lambda i, j: (layer, 0, j)),
                  *[pl.BlockSpec(memory_space=pl.ANY) for _ in caches]],
        out_specs=(pl.BlockSpec((tm, tn), lambda i, j: (i, j)), *[cache_spec(c) for c in caches]),
        input_output_aliases={n_in + k: 1 + k for k in range(len(caches))},
        compiler_params=_cparams("parallel", "arbitrary"),
        name="in_proj",
    )(h, w_all, *caches)


def _lam_value(lam_ref, lam_init):
    lv = lam_ref[...]
    d1 = jnp.sum(lv[0:1] * lv[1:2], axis=-1, keepdims=True)
    d2 = jnp.sum(lv[2:3] * lv[3:4], axis=-1, keepdims=True)
    return jnp.exp(d1) - jnp.exp(d2) + lam_init


def _softmax(s):
    e = jnp.exp(s - jnp.max(s, axis=-1, keepdims=True))
    return e / jnp.sum(e, axis=-1, keepdims=True)


def _diff_attend(q, kb, vb, lam):
    lane = lax.broadcasted_iota(jnp.int32, q.shape, 1)
    q1 = jnp.where(lane < A_QK, q, 0.0).astype(BF16)
    q2 = jnp.where(lane >= A_QK, q, 0.0).astype(BF16)
    scale = A_QK ** -0.5
    s1 = _dot_nt(q1, kb) * scale
    s2 = _dot_nt(q2, kb) * scale
    p = _softmax(s1) - lam * _softmax(s2)
    return _dot(p.astype(BF16), vb)


KEY_CHUNK = 256


def _diff_attend_chunked(q, kr_ref, vb_ref, lam, kc):
    lane = lax.broadcasted_iota(jnp.int32, q.shape, 1)
    qs = q * (A_QK ** -0.5)
    q1 = jnp.where(lane < A_QK, qs, 0.0).astype(BF16)
    q2 = jnp.where(lane >= A_QK, qs, 0.0).astype(BF16)
    chunks = [slice(c * kc, (c + 1) * kc) for c in range(kr_ref.shape[0] // kc)]

    def fold(x, op):
        out = x[:, :LANES]
        for j in range(1, kc // LANES):
            out = op(out, x[:, j * LANES:(j + 1) * LANES])
        return out

    m1 = m2 = None
    for cs in chunks:
        kb = kr_ref[cs, :]
        c1 = fold(_dot_nt(q1, kb), jnp.maximum)
        c2 = fold(_dot_nt(q2, kb), jnp.maximum)
        m1 = c1 if m1 is None else jnp.maximum(m1, c1)
        m2 = c2 if m2 is None else jnp.maximum(m2, c2)
    m1 = jnp.max(m1, axis=-1, keepdims=True)
    m2 = jnp.max(m2, axis=-1, keepdims=True)
    l1 = l2 = o1 = o2 = 0.0
    for cs in chunks:
        kb = kr_ref[cs, :]
        vb = vb_ref[cs, :]
        e1 = jnp.exp(_dot_nt(q1, kb) - m1)
        e2 = jnp.exp(_dot_nt(q2, kb) - m2)
        l1 = l1 + fold(e1, jnp.add)
        l2 = l2 + fold(e2, jnp.add)
        o1 = o1 + _dot(e1.astype(BF16), vb)
        o2 = o2 + _dot(e2.astype(BF16), vb)
    l1 = jnp.sum(l1, axis=-1, keepdims=True)
    l2 = jnp.sum(l2, axis=-1, keepdims=True)
    return o1 / l1 - lam * (o2 / l2)


def _sub_ln(o, g, lam_init):
    r = lax.rsqrt(jnp.mean(o * o, axis=-1, keepdims=True) + SUBLN_EPS)
    return (o * r) * g * (1.0 - lam_init)


def _rope(x, cos, sin_signed, half):
    if 2 * half == LANES:
        swapped = pltpu.roll(x, half, axis=1)
    else:
        lane = lax.broadcasted_iota(jnp.int32, x.shape, 1)
        first = (lane % (2 * half)) < half
        swapped = jnp.where(first, pltpu.roll(x, LANES - half, axis=1),
                            pltpu.roll(x, half, axis=1))
    return x * cos + swapped * sin_signed


def _rope_tables(n_tokens, head_dim):
    rows = n_tokens // GRID_W
    row = np.repeat(np.arange(rows, dtype=np.float64), GRID_W)
    col = np.tile(np.arange(GRID_W, dtype=np.float64), rows)
    n_freq = head_dim // 4
    inv = np.power(ROPE_BASE, -np.arange(n_freq, dtype=np.float64) / n_freq)
    ang = np.concatenate([row[:, None] * inv, col[:, None] * inv], axis=-1)
    cos, sin = np.cos(ang), np.sin(ang)
    reps = LANES // head_dim
    cos_t = np.tile(np.concatenate([cos, cos], axis=-1), (1, reps))
    sin_t = np.tile(np.concatenate([-sin, sin], axis=-1), (1, reps))
    return cos_t.astype(np.float32), sin_t.astype(np.float32)


def _dft_cos_sin(n):
    jk = np.outer(np.arange(n), np.arange(n)) % n
    ang = jk.astype(np.float64) * (2.0 * math.pi / n)
    return np.cos(ang) * n ** -0.5, np.sin(ang) * n ** -0.5


def _dft_table(n):
    c, s = _dft_cos_sin(n)
    return np.concatenate([c, -s], axis=1).astype(np.float32)


def _dft_channel_table():
    c, s = _dft_cos_sin(B_GD)
    return np.concatenate([c, s], axis=1).astype(np.float32)


def _fourier_stage1(u, csh, csl):
    uh, ul = _split(u)
    t = _dot3(uh, ul, csh, csl)
    return jnp.concatenate([t[:, :B_GD], t[:, B_GD:]], axis=0)


PROMPT_BATCHES_PER_STEP = 2


def _prompt_mixers(proj, lam_l, subln_l, sink_l, w2, csh, csl, wf, lam_init, nbatch, n):
    t = nbatch * n
    bps = math.gcd(PROMPT_BATCHES_PER_STEP, nbatch)

    def body(p_ref, lam_ref, g_ref, sink_ref, w2_ref, csh_ref, csl_ref, wf_ref,
             oa_ref, ob_ref, oc_ref):
        lam = _lam_value(lam_ref, lam_init)
        scale = C_HD ** -0.5
        for bb in range(bps):
            rows = slice(bb * n, (bb + 1) * n)
            for h in range(A_HEADS):
                q = p_ref[rows, COL_AQ + h * A_V:COL_AQ + (h + 1) * A_V]
                kb = p_ref[rows, COL_AK + h * A_V:COL_AK + (h + 1) * A_V].astype(BF16)
                vb = p_ref[rows, COL_AV + h * A_V:COL_AV + (h + 1) * A_V].astype(BF16)
                o = _diff_attend(q, kb, vb, lam)
                oa_ref[rows, h * A_V:(h + 1) * A_V] = _sub_ln(o, g_ref[...], lam_init).astype(BF16)

            acc = None
            for g in range(B_GROUPS):
                u = p_ref[rows, COL_BU + g * B_GD:COL_BU + (g + 1) * B_GD]
                tb = _fourier_stage1(u, csh_ref[...], csl_ref[...]).astype(BF16)
                f = _dot(w2_ref[...], tb)
                part = _dot(f.astype(BF16), wf_ref[g * B_GD:(g + 1) * B_GD, :])
                acc = part if acc is None else acc + part
            ob_ref[rows, :] = acc.astype(BF16)

            for hq in range(C_HEADS):
                kvh = hq // C_GROUP
                q = p_ref[rows, COL_CQ + hq * C_HD:COL_CQ + (hq + 1) * C_HD].astype(BF16)
                kb = p_ref[rows, COL_CK + kvh * C_HD:COL_CK + (kvh + 1) * C_HD].astype(BF16)
                vb = p_ref[rows, COL_CV + kvh * C_HD:COL_CV + (kvh + 1) * C_HD].astype(BF16)
                s = _dot_nt(q, kb) * scale
                snk = sink_ref[hq]
                m = jnp.maximum(jnp.max(s, axis=-1, keepdims=True), snk)
                e = jnp.exp(s - m)
                den = jnp.sum(e, axis=-1, keepdims=True) + jnp.exp(snk - m)
                o = _dot((e / den).astype(BF16), vb)
                oc_ref[rows, hq * C_HD:(hq + 1) * C_HD] = o.astype(BF16)

    full = lambda shape: pl.BlockSpec(shape, lambda b: (0,) * len(shape))
    out = jax.ShapeDtypeStruct((t, GROUP_WIDTH), BF16)
    ospec = pl.BlockSpec((bps * n, GROUP_WIDTH), lambda b: (b, 0))
    return pl.pallas_call(
        body,
        out_shape=(out, out, out),
        grid=(nbatch // bps,),
        in_specs=[pl.BlockSpec((bps * n, proj.shape[1]), lambda b: (b, 0)),
                  full(lam_l.shape), full(subln_l.shape),
                  pl.BlockSpec(memory_space=pltpu.SMEM),
                  full(w2.shape), full(csh.shape), full(csl.shape), full(wf.shape)],
        out_specs=(ospec, ospec, ospec),
        compiler_params=_cparams("parallel"),
        name="prompt_mixers",
    )(proj, lam_l, subln_l, sink_l, w2, csh, csl, wf)


def _sample_diff_attn(proj, cache_k, cache_v, layer, cos_t, sin_t, lam_l, subln_l,
                      lam_init, tp, nbatch, ns, tq):
    past = cache_k.shape[2]
    nk = ns + past
    qb0 = tp // tq
    qpb = ns // tq
    kb0 = tp // ns

    def body(q_ref, k_ref, v_ref, ck_ref, cv_ref, cq_ref, sq_ref, cosk_ref, sink_ref,
             lam_ref, g_ref, o_ref, kr_ref, vb_ref):
        @pl.when(pl.program_id(2) == 0)
        def _():
            kr_ref[0:ns, :] = _rope(k_ref[...], cosk_ref[...], sink_ref[...], A_QK // 2).astype(BF16)
            kr_ref[ns:nk, :] = ck_ref[...].astype(BF16)
            vb_ref[0:ns, :] = v_ref[...].astype(BF16)
            vb_ref[ns:nk, :] = cv_ref[...].astype(BF16)

        lam = _lam_value(lam_ref, lam_init)
        q = _rope(q_ref[...], cq_ref[...], sq_ref[...], A_QK // 2)
        o = _diff_attend_chunked(q, kr_ref, vb_ref, lam, KEY_CHUNK)
        o_ref[...] = _sub_ln(o, g_ref[...], lam_init).astype(BF16)

    assert nk % KEY_CHUNK == 0
    cache_spec = pl.BlockSpec((None, None, past, A_V), lambda b, h, i: (b, layer, 0, h))
    full = lambda shape: pl.BlockSpec(shape, lambda b, h, i: (0,) * len(shape))
    return pl.pallas_call(
        body,
        out_shape=jax.ShapeDtypeStruct((nbatch * ns, GROUP_WIDTH), BF16),
        grid=(nbatch, A_HEADS, qpb),
        in_specs=[pl.BlockSpec((tq, A_V), lambda b, h, i: (qb0 + b * qpb + i, COL_AQ // A_V + h)),
                  pl.BlockSpec((ns, A_V), lambda b, h, i: (kb0 + b, COL_AK // A_V + h)),
                  pl.BlockSpec((ns, A_V), lambda b, h, i: (kb0 + b, COL_AV // A_V + h)),
                  cache_spec, cache_spec,
                  pl.BlockSpec((tq, LANES), lambda b, h, i: (i, 0)),
                  pl.BlockSpec((tq, LANES), lambda b, h, i: (i, 0)),
                  full((ns, LANES)), full((ns, LANES)),
                  full(lam_l.shape), full(subln_l.shape)],
        out_specs=pl.BlockSpec((tq, A_V), lambda b, h, i: (b * qpb + i, h)),
        scratch_shapes=[pltpu.VMEM((nk, A_V), BF16), pltpu.VMEM((nk, A_V), BF16)],
        compiler_params=_cparams("parallel", "parallel", "arbitrary"),
        name="sample_diff_attn",
    )(proj, proj, proj, cache_k, cache_v, cos_t, sin_t, cos_t, sin_t, lam_l, subln_l)


def _sample_window_attn(proj, cache_k, cache_v, layer, cos_t, sin_t, sink_l,
                        tp, nbatch, ns):
    w = C_WINDOW
    past = cache_k.shape[2]
    nblk = ns // w
    qb0 = tp // w
    kb0 = tp // ns
    gw = C_GROUP * C_HD
    scale = C_HD ** -0.5

    def body(q_ref, k_ref, v_ref, ck_ref, cv_ref, cq_ref, sq_ref, cosk_ref, sink_ref,
             snk_ref, o_ref, kr_ref, vb_ref):
        kvh = pl.program_id(1)
        i = pl.program_id(2)

        @pl.when(i == 0)
        def _():
            kr_ref[...] = _rope(k_ref[...], cosk_ref[...], sink_ref[...], C_HD // 2).astype(BF16)
            vb_ref[...] = v_ref[...].astype(BF16)

        qs = [_rope(q_ref[:, g * C_HD:(g + 1) * C_HD], cq_ref[...], sq_ref[...], C_HD // 2)
              for g in range(C_GROUP)]
        qq = jnp.concatenate(qs, axis=0).astype(BF16)
        start = pl.multiple_of(jnp.clip((i - 1) * w, 0, ns - 3 * w), w)
        kw = kr_ref[pl.ds(start, 3 * w), :]
        vw = vb_ref[pl.ds(start, 3 * w), :]
        s_loc = _dot_nt(qq, kw) * scale
        rows = lax.broadcasted_iota(jnp.int32, s_loc.shape, 0)
        cols = lax.broadcasted_iota(jnp.int32, s_loc.shape, 1)
        qpos = i * w + (rows & (w - 1))
        kpos = start + cols
        s_loc = jnp.where(jnp.abs(kpos - qpos) <= w, s_loc, NEG)
        s_ctx = _dot_nt(qq, ck_ref[...].astype(BF16)) * scale
        rcol = lax.broadcasted_iota(jnp.int32, (C_GROUP * w, 1), 0)
        snk = jnp.zeros((C_GROUP * w, 1), F32)
        for g in range(C_GROUP):
            snk = jnp.where(rcol >= g * w, snk_ref[kvh * C_GROUP + g], snk)
        m = jnp.maximum(jnp.maximum(jnp.max(s_loc, axis=-1, keepdims=True),
                                    jnp.max(s_ctx, axis=-1, keepdims=True)), snk)
        e_loc = jnp.exp(s_loc - m)
        e_ctx = jnp.exp(s_ctx - m)
        den = (jnp.sum(e_loc, axis=-1, keepdims=True) + jnp.sum(e_ctx, axis=-1, keepdims=True)
               + jnp.exp(snk - m))
        o = (_dot((e_loc / den).astype(BF16), vw)
             + _dot((e_ctx / den).astype(BF16), cv_ref[...].astype(BF16)))
        for g in range(C_GROUP):
            o_ref[:, g * C_HD:(g + 1) * C_HD] = o[g * w:(g + 1) * w, :].astype(BF16)

    cache_spec = pl.BlockSpec((None, None, past, C_HD), lambda b, h, i: (b, layer, 0, h))
    full = lambda shape: pl.BlockSpec(shape, lambda b, h, i: (0,) * len(shape))
    return pl.pallas_call(
        body,
        out_shape=jax.ShapeDtypeStruct((nbatch * ns, GROUP_WIDTH), BF16),
        grid=(nbatch, C_KV_HEADS, nblk),
        in_specs=[pl.BlockSpec((w, gw), lambda b, h, i: (qb0 + b * nblk + i, COL_CQ // gw + h)),
                  pl.BlockSpec((ns, C_HD), lambda b, h, i: (kb0 + b, COL_CK // C_HD + h)),
                  pl.BlockSpec((ns, C_HD), lambda b, h, i: (kb0 + b, COL_CV // C_HD + h)),
                  cache_spec, cache_spec,
                  pl.BlockSpec((w, LANES), lambda b, h, i: (i, 0)),
                  pl.BlockSpec((w, LANES), lambda b, h, i: (i, 0)),
                  full((ns, LANES)), full((ns, LANES)),
                  pl.BlockSpec(memory_space=pltpu.SMEM)],
        out_specs=pl.BlockSpec((w, gw), lambda b, h, i: (b * nblk + i, h)),
        scratch_shapes=[pltpu.VMEM((ns, C_HD), BF16), pltpu.VMEM((ns, C_HD), BF16)],
        compiler_params=_cparams("parallel", "parallel", "arbitrary"),
        name="sample_window_attn",
    )(proj, proj, proj, cache_k, cache_v, cos_t, sin_t, cos_t, sin_t, sink_l)


def _sample_fourier(proj, w2, csh, csl, wf, tp, nbatch, ns, tr):
    rpb = ns // tr
    kb0 = tp // ns

    def body(u_ref, w2_ref, csh_ref, csl_ref, wf_ref, o_ref, t_ref):
        @pl.when(pl.program_id(1) == 0)
        def _():
            for g in range(B_GROUPS):
                sl = slice(g * B_GD, (g + 1) * B_GD)
                t_ref[:, sl] = _fourier_stage1(u_ref[:, sl], csh_ref[...], csl_ref[...]).astype(BF16)

        f = _dot(w2_ref[...], t_ref[...])
        o_ref[...] = _dot(f.astype(BF16), wf_ref[...]).astype(BF16)

    full = lambda shape: pl.BlockSpec(shape, lambda b, r: (0,) * len(shape))
    return pl.pallas_call(
        body,
        out_shape=jax.ShapeDtypeStruct((nbatch * ns, GROUP_WIDTH), BF16),
        grid=(nbatch, rpb),
        in_specs=[pl.BlockSpec((ns, GROUP_WIDTH), lambda b, r: (kb0 + b, COL_BU // GROUP_WIDTH)),
                  pl.BlockSpec((tr, 2 * ns), lambda b, r: (r, 0)),
                  full(csh.shape), full(csl.shape), full(wf.shape)],
        out_specs=pl.BlockSpec((tr, GROUP_WIDTH), lambda b, r: (b * rpb + r, 0)),
        scratch_shapes=[pltpu.VMEM((2 * ns, GROUP_WIDTH), BF16)],
        compiler_params=_cparams("parallel", "arbitrary"),
        name="sample_fourier",
    )(proj, w2, csh, csl, wf)


def _spatial_gate(proj, g, b, ws, bst, tr):
    t = proj.shape[0]

    def body(u_ref, v_ref, g_ref, b_ref, ws_ref, bs_ref, o_ref):
        vn = (_ln_rows(v_ref[...]) * g_ref[...] + b_ref[...]).astype(BF16)
        for c in range(tr // CHUNK):
            rs = slice(c * CHUNK, (c + 1) * CHUNK)
            for h in range(D_HEADS):
                cs = slice(h * D_HD, (h + 1) * D_HD)
                mixed = _dot(ws_ref[h], vn[rs, cs]) + bs_ref[:, h:h + 1]
                o_ref[rs, cs] = (u_ref[rs, cs] * mixed).astype(BF16)

    full = lambda shape: pl.BlockSpec(shape, lambda i: (0,) * len(shape))
    return pl.pallas_call(
        body,
        out_shape=jax.ShapeDtypeStruct((t, GROUP_WIDTH), BF16),
        grid=(t // tr,),
        in_specs=[pl.BlockSpec((tr, GROUP_WIDTH), lambda i: (i, COL_DU // GROUP_WIDTH)),
                  pl.BlockSpec((tr, GROUP_WIDTH), lambda i: (i, COL_DV // GROUP_WIDTH)),
                  full(g.shape), full(b.shape), full(ws.shape), full(bst.shape)],
        out_specs=pl.BlockSpec((tr, GROUP_WIDTH), lambda i: (i, 0)),
        compiler_params=_cparams("parallel"),
        name="spatial_gate",
    )(proj, proj, g, b, ws, bst)


def _post_ln(x, f, gate, g, b):
    return _ln_rows(ALPHA * x + gate * f) * g + b


OUT_SUB_ROWS = 256
ROUTER_TERM_LANES = 8


def _pack_router(w_router_l, b_router_l):
    d = w_router_l.shape[0]
    assert N_EXPERTS <= ROUTER_TERM_LANES
    wp = jnp.zeros((d, LANES), BF16)
    for k, term in enumerate(_split3(w_router_l)):
        wp = wp.at[:, k * ROUTER_TERM_LANES:k * ROUTER_TERM_LANES + N_EXPERTS].set(term)
    br = jnp.zeros((1, LANES), F32).at[0, :N_EXPERTS].set(b_router_l)
    return wp, br


def _out_proj(abc_p, abc_s, od, w_all, layer, x, mods_l, g1, b1, router, tp, ns, tm):
    t, d = x.shape
    with_router = router is not None
    np_tiles = tp // tm
    sub = min(OUT_SUB_ROWS, tm)

    def body(ap_ref, bp_ref, cp_ref, as_ref, bs_ref, cs_ref, od_ref, w_ref, x_ref, m_ref,
             g_ref, b_ref, *rest):
        if with_router:
            wr_ref, br_ref, x1_ref, h2_ref, lg_ref, cat_ref = rest
        else:
            x1_ref, h2_ref, cat_ref = rest
        gw = GROUP_WIDTH
        is_ctx = pl.program_id(0) < np_tiles
        cat_ref[:, 0:gw] = jnp.where(is_ctx, ap_ref[...], as_ref[...])
        cat_ref[:, gw:2 * gw] = jnp.where(is_ctx, bp_ref[...], bs_ref[...])
        cat_ref[:, 2 * gw:3 * gw] = jnp.where(is_ctx, cp_ref[...], cs_ref[...])
        cat_ref[:, 3 * gw:4 * gw] = od_ref[...]
        for r in range(tm // sub):
            rs = slice(r * sub, (r + 1) * sub)
            mix = _dot(cat_ref[rs, :], w_ref[...])
            x1 = _post_ln(x_ref[rs, :], mix, m_ref[0, 2:3, :], g_ref[...], b_ref[...])
            x1_ref[rs, :] = x1
            h2 = _mod_ln(x1, m_ref, 3)
            h2_ref[rs, :] = h2.astype(h2_ref.dtype)
            if with_router:
                h0, h1, h2b = _split3(h2)
                p0, p1, p2 = (_dot(hk, wr_ref[...]) for hk in (h0, h1, h2b))
                left = lambda p, k: pltpu.roll(p, LANES - k * ROUTER_TERM_LANES, axis=1)
                lg = p0 + (left(p0, 1) + p1) + (left(p0, 2) + left(p1, 1) + p2)
                lg_ref[rs, :] = lg + br_ref[...]

    row = functools.partial(_mod_row, tm=tm, tp=tp, ns=ns)
    full = lambda shape: pl.BlockSpec(shape, lambda i: (0,) * len(shape))
    tile = lambda n: pl.BlockSpec((tm, n), lambda i: (i, 0))
    ctx_tile = pl.BlockSpec((tm, GROUP_WIDTH), lambda i: (jnp.minimum(i, np_tiles - 1), 0))
    lat_tile = pl.BlockSpec((tm, GROUP_WIDTH), lambda i: (jnp.maximum(i - np_tiles, 0), 0))
    w_spec = pl.BlockSpec((None,) + w_all.shape[1:], lambda i: (layer, 0, 0),
                          pipeline_mode=pl.Buffered(1))
    in_specs = [ctx_tile] * 3 + [lat_tile] * 3 + [
        tile(GROUP_WIDTH), w_spec, tile(d),
        pl.BlockSpec((1, 6, d), lambda i: (row(i), 0, 0)), full(g1.shape), full(b1.shape)]
    args = [*abc_p, *abc_s, od, w_all, x, mods_l, g1, b1]
    out_shape = [jax.ShapeDtypeStruct((t, d), F32),
                 jax.ShapeDtypeStruct((t, d), F32 if with_router else BF16)]
    out_specs = [tile(d), tile(d)]
    if with_router:
        in_specs += [full(r.shape) for r in router]
        args += list(router)
        out_shape.append(jax.ShapeDtypeStruct((t, LANES), F32))
        out_specs.append(tile(LANES))
    return pl.pallas_call(
        body, out_shape=tuple(out_shape), grid=(t // tm,),
        in_specs=in_specs, out_specs=tuple(out_specs),
        scratch_shapes=[pltpu.VMEM((tm, 4 * GROUP_WIDTH), BF16)],
        compiler_params=_cparams("parallel"),
        name="out_proj",
    )(*args)


def _ffn_up(h2, wg, wu, layer, moe_w, moe_layer, tm, tn):
    t, d = h2.shape
    dff = wg.shape[2]
    nj, ni = dff // tn, t // tm
    _, n_exp, _, dffe = moe_w[0].shape
    rows, cols = d // ni, dffe // nj
    assert rows * ni == d and cols * nj == dffe and rows % 16 == 0 and cols % LANES == 0

    def body(h_ref, wg_ref, wu_ref, mg_ref, mu_ref, md_ref, a_ref, og_ref, ou_ref, od_ref):
        h = h_ref[...]
        a_ref[...] = (_silu(_dot(h, wg_ref[...])) * _dot(h, wu_ref[...])).astype(BF16)
        for src, dst in ((mg_ref, og_ref), (mu_ref, ou_ref), (md_ref, od_ref)):
            dst[...] = src[...].astype(BF16)

    up_in = pl.BlockSpec((None, n_exp, rows, cols), lambda j, i: (moe_layer, 0, i, j))
    up_out = pl.BlockSpec((n_exp, rows, cols), lambda j, i: (0, i, j))
    down_in = pl.BlockSpec((None, n_exp, cols, rows), lambda j, i: (moe_layer, 0, j, i))
    down_out = pl.BlockSpec((n_exp, cols, rows), lambda j, i: (0, j, i))
    return pl.pallas_call(
        body,
        out_shape=(jax.ShapeDtypeStruct((t, dff), BF16),
                   jax.ShapeDtypeStruct((n_exp, d, dffe), BF16),
                   jax.ShapeDtypeStruct((n_exp, d, dffe), BF16),
                   jax.ShapeDtypeStruct((n_exp, dffe, d), BF16)),
        grid=(nj, ni),
        in_specs=[pl.BlockSpec((tm, d), lambda j, i: (i, 0)),
                  pl.BlockSpec((None, d, tn), lambda j, i: (layer, 0, j)),
                  pl.BlockSpec((None, d, tn), lambda j, i: (layer, 0, j)),
                  up_in, up_in, down_in],
        out_specs=(pl.BlockSpec((tm, tn), lambda j, i: (i, j)), up_out, up_out, down_out),
        compiler_params=_cparams("parallel", "parallel"),
        name="ffn_up",
    )(h2, wg, wu, *moe_w)


def _ffn_down(a, wd, layer, x1, mods_l, mods_next, g2, b2, tp, ns, tm):
    t, d = x1.shape
    dff = a.shape[1]

    def body(a_ref, wd_ref, x_ref, m_ref, mn_ref, g_ref, b_ref, o_ref, hn_ref):
        f = _dot(a_ref[...], wd_ref[...])
        x2 = _post_ln(x_ref[...], f, m_ref[0, 5:6, :], g_ref[...], b_ref[...])
        o_ref[...] = x2
        hn_ref[...] = _mod_ln(x2, mn_ref, 0).astype(BF16)

    row = functools.partial(_mod_row, tm=tm, tp=tp, ns=ns)
    full = lambda shape: pl.BlockSpec(shape, lambda i: (0,) * len(shape))
    mod_spec = pl.BlockSpec((1, 6, d), lambda i: (row(i), 0, 0))
    tile = pl.BlockSpec((tm, d), lambda i: (i, 0))
    return pl.pallas_call(
        body,
        out_shape=(jax.ShapeDtypeStruct((t, d), F32), jax.ShapeDtypeStruct((t, d), BF16)),
        grid=(t // tm,),
        in_specs=[pl.BlockSpec((tm, dff), lambda i: (i, 0)),
                  pl.BlockSpec((None, dff, d), lambda i: (layer, 0, 0), pipeline_mode=pl.Buffered(1)),
                  tile, mod_spec, mod_spec, full(g2.shape), full(b2.shape)],
        out_specs=(tile, tile),
        compiler_params=_cparams("parallel"),
        name="ffn_down",
    )(a, wd, x1, mods_l, mods_next, g2, b2)


ROUTE_I1, ROUTE_I2, ROUTE_W1, ROUTE_W2, ROUTE_R1, ROUTE_R2 = range(6)


def _route(logits, tr):
    t = logits.shape[0]

    def body(lg_ref, route_ref, cnt_ref, carry_ref):
        @pl.when(pl.program_id(0) == 0)
        def _():
            carry_ref[...] = jnp.zeros_like(carry_ref)

        lane = lax.broadcasted_iota(jnp.int32, (tr, LANES), 1)
        lane_f = lane.astype(F32)
        lg = jnp.where(lane < N_EXPERTS, lg_ref[...], -jnp.inf)
        m1 = jnp.max(lg, axis=-1, keepdims=True)
        i1 = jnp.min(jnp.where(lg == m1, lane_f, float(LANES)), axis=-1, keepdims=True)
        lg2 = jnp.where(lane_f == i1, -jnp.inf, lg)
        m2 = jnp.max(lg2, axis=-1, keepdims=True)
        i2 = jnp.min(jnp.where(lg2 == m2, lane_f, float(LANES)), axis=-1, keepdims=True)
        e2 = jnp.exp(m2 - m1)
        den = 1.0 + e2
        oh1 = lane_f == i1
        oh2 = lane_f == i2
        sel = jnp.where(oh1 | oh2, 1.0, 0.0)
        ri = lax.broadcasted_iota(jnp.int32, (tr, tr), 0)
        ci = lax.broadcasted_iota(jnp.int32, (tr, tr), 1)
        tri = jnp.where(ri > ci, 1.0, 0.0).astype(BF16)
        prefix = _dot(tri, sel.astype(BF16)) + carry_ref[0:1, :]
        r1 = jnp.sum(jnp.where(oh1, prefix, 0.0), axis=-1, keepdims=True)
        r2 = jnp.sum(jnp.where(oh2, prefix, 0.0), axis=-1, keepdims=True)
        carry_ref[0:1, :] = carry_ref[0:1, :] + jnp.sum(sel, axis=0, keepdims=True)
        out = jnp.zeros((tr, LANES), F32)
        for idx, val in ((ROUTE_I1, i1), (ROUTE_I2, i2), (ROUTE_W1, 1.0 / den),
                         (ROUTE_W2, e2 / den), (ROUTE_R1, r1), (ROUTE_R2, r2)):
            out = jnp.where(lane == idx, val, out)
        route_ref[...] = out
        cnt_ref[...] = carry_ref[...]

    return pl.pallas_call(
        body,
        out_shape=(jax.ShapeDtypeStruct((t, LANES), F32), jax.ShapeDtypeStruct((8, LANES), F32)),
        grid=(t // tr,),
        in_specs=[pl.BlockSpec((tr, LANES), lambda i: (i, 0))],
        out_specs=(pl.BlockSpec((tr, LANES), lambda i: (i, 0)),
                   pl.BlockSpec((8, LANES), lambda i: (0, 0))),
        scratch_shapes=[pltpu.VMEM((8, LANES), F32)],
        compiler_params=_cparams("arbitrary"),
        name="route",
    )(logits)


ROW_COPY_UNROLL = 8


def _row_sources(pos1, pos2, fill_from, fill_to, n_rows):
    t = pos1.shape[0]

    def body(p1_ref, p2_ref, lo_ref, hi_ref, src_ref):
        def clear(r, carry):
            src_ref[r] = 0
            return carry

        def put(tok, carry):
            src_ref[p1_ref[tok]] = tok
            src_ref[p2_ref[tok]] = tok
            return carry

        lax.fori_loop(0, t, put, 0, unroll=ROW_COPY_UNROLL)
        for e in range(N_EXPERTS):
            lax.fori_loop(lo_ref[e], hi_ref[e], clear, 0)
        lax.fori_loop(hi_ref[N_EXPERTS - 1], n_rows, clear, 0)

    smem = pl.BlockSpec(memory_space=pltpu.SMEM)
    return pl.pallas_call(
        body,
        out_shape=jax.ShapeDtypeStruct((n_rows,), jnp.int32),
        in_specs=[smem] * 4, out_specs=smem,
        name="row_sources",
    )(pos1, pos2, fill_from, fill_to)


def _experts(tile_expert, n_used, src, h, wg, wu, wd, tm, ck):
    d = h.shape[1]
    nt = src.shape[0] // tm
    dff = wg.shape[2]

    def row_copy(src_ref, h_ref, xg_ref, sem, tile, slot, r):
        return pltpu.make_async_copy(h_ref.at[pl.ds(src_ref[tile * tm + r], 1)],
                                     xg_ref.at[slot, pl.ds(r, 1)], sem.at[slot])

    def body(te_ref, nu_ref, src_ref, h_ref, wg_ref, wu_ref, wd_ref, y_ref, xg_ref, a_ref, sem):
        del te_ref
        i = pl.program_id(0)
        nu = nu_ref[0]

        def issue(tile, slot):
            def step(r, carry):
                row_copy(src_ref, h_ref, xg_ref, sem, tile, slot, r).start()
                return carry
            lax.fori_loop(0, tm, step, 0, unroll=ROW_COPY_UNROLL)

        def drain(tile, slot):
            del tile
            pltpu.make_async_copy(h_ref.at[pl.ds(0, tm)], xg_ref.at[slot], sem.at[slot]).wait()

        @pl.when(i == 0)
        def _():
            issue(0, 0)

        @pl.when(i < nu)
        def _():
            slot = i & 1
            drain(i, slot)
            x = xg_ref[slot].astype(BF16)
            nxt = jnp.minimum(i + 1, nu - 1)
            nc = dff // ck
            for c in range(nc):
                cs = slice(c * ck, (c + 1) * ck)
                a_ref[:, cs] = (_silu(_dot(x, wg_ref[:, cs])) * _dot(x, wu_ref[:, cs])).astype(BF16)
                for r in range(c * tm // nc, (c + 1) * tm // nc):
                    row_copy(src_ref, h_ref, xg_ref, sem, nxt, 1 - slot, r).start(priority=r % 2)
            y_ref[...] = _dot(a_ref[...], wd_ref[...])

            @pl.when(i == nu - 1)
            def _():
                drain(nxt, 1 - slot)

        @pl.when(i >= nu)
        def _():
            y_ref[...] = jnp.zeros_like(y_ref)

    def expert_weights(shape):
        return pl.BlockSpec((None,) + shape, lambda i, te, nu, src: (te[i], 0, 0),
                            pipeline_mode=pl.Buffered(1))

    return pl.pallas_call(
        body,
        out_shape=jax.ShapeDtypeStruct((nt * tm, d), F32),
        grid_spec=pltpu.PrefetchScalarGridSpec(
            num_scalar_prefetch=3, grid=(nt,),
            in_specs=[pl.BlockSpec(memory_space=pl.ANY),
                      expert_weights((d, dff)), expert_weights((d, dff)),
                      expert_weights((dff, d))],
            out_specs=pl.BlockSpec((tm, d), lambda i, te, nu, src: (i, 0)),
            scratch_shapes=[pltpu.VMEM((2, tm, d), F32), pltpu.VMEM((tm, dff), BF16),
                            pltpu.SemaphoreType.DMA((2,))]),
        compiler_params=_cparams("arbitrary"),
        name="experts",
    )(tile_expert, n_used, src, h, wg, wu, wd)


COMBINE_ROWS = 32


def _combine(pos1, pos2, ys, route, x1, mods_l, mods_next, g2, b2, tp, ns, tc):
    t, d = x1.shape
    is_last = mods_next is None
    np_tiles = tp // tc

    def row_copies(p1_ref, p2_ref, ys_ref, y1_ref, y2_ref, sem, tile, slot, r):
        tok = tile * tc + r
        return (pltpu.make_async_copy(ys_ref.at[pl.ds(p1_ref[tok], 1)],
                                      y1_ref.at[slot, pl.ds(r, 1)], sem.at[0, slot]),
                pltpu.make_async_copy(ys_ref.at[pl.ds(p2_ref[tok], 1)],
                                      y2_ref.at[slot, pl.ds(r, 1)], sem.at[1, slot]))

    def body(p1_ref, p2_ref, ys_ref, rt_ref, x_ref, m_ref, *rest):
        if is_last:
            g_ref, b_ref, yp_ref, yl_ref, y1_ref, y2_ref, sem, o_ref = rest
        else:
            mn_ref, g_ref, b_ref, o_ref, hn_ref, y1_ref, y2_ref, sem = rest
        j = pl.program_id(0)

        def issue(tile, slot):
            def step(r, carry):
                for cp in row_copies(p1_ref, p2_ref, ys_ref, y1_ref, y2_ref, sem, tile, slot, r):
                    cp.start()
                return carry
            lax.fori_loop(0, tc, step, 0, unroll=ROW_COPY_UNROLL)

        def drain(slot):
            for k, buf in enumerate((y1_ref, y2_ref)):
                pltpu.make_async_copy(ys_ref.at[pl.ds(0, tc)], buf.at[slot], sem.at[k, slot]).wait()

        @pl.when(j == 0)
        def _():
            issue(0, 0)

        slot = j & 1
        drain(slot)
        last = pl.num_programs(0) - 1
        nxt = jnp.minimum(j + 1, last)
        for blk in range(tc // COMBINE_ROWS):
            rs = slice(blk * COMBINE_ROWS, (blk + 1) * COMBINE_ROWS)
            rt = rt_ref[rs, :]
            f = (rt[:, ROUTE_W1:ROUTE_W1 + 1] * y1_ref[slot, rs, :]
                 + rt[:, ROUTE_W2:ROUTE_W2 + 1] * y2_ref[slot, rs, :])
            x2 = _post_ln(x_ref[rs, :], f, m_ref[0, 5:6, :], g_ref[...], b_ref[...])
            o_ref[rs, :] = x2
            if not is_last:
                hn_ref[rs, :] = _mod_ln(x2, mn_ref, 0).astype(BF16)
            for r in range(blk * COMBINE_ROWS, (blk + 1) * COMBINE_ROWS):
                copies = row_copies(p1_ref, p2_ref, ys_ref, y1_ref, y2_ref, sem, nxt, 1 - slot, r)
                for queue, cp in enumerate(copies):
                    cp.start(priority=queue)

        @pl.when(j == last)
        def _():
            drain(1 - slot)

        if is_last:
            @pl.when(j < np_tiles)
            def _():
                yp_ref[...] = o_ref[...]

            @pl.when(j >= np_tiles)
            def _():
                yl_ref[...] = o_ref[...]

    row = functools.partial(_mod_row, tm=tc, tp=tp, ns=ns)
    full = lambda shape: pl.BlockSpec(shape, lambda i, p1, p2: (0,) * len(shape))
    tile = pl.BlockSpec((tc, d), lambda i, p1, p2: (i, 0))
    mod_spec = pl.BlockSpec((1, 6, d), lambda i, p1, p2: (row(i), 0, 0))
    in_specs = [pl.BlockSpec(memory_space=pl.ANY),
                pl.BlockSpec((tc, LANES), lambda i, p1, p2: (i, 0)), tile, mod_spec]
    args = [pos1, pos2, ys, route, x1, mods_l]
    scratch = [pltpu.VMEM((2, tc, d), F32), pltpu.VMEM((2, tc, d), F32),
               pltpu.SemaphoreType.DMA((2, 2))]
    if is_last:
        out_shape = (jax.ShapeDtypeStruct((tp, d), F32), jax.ShapeDtypeStruct((t - tp, d), F32))
        out_specs = (pl.BlockSpec((tc, d), lambda i, p1, p2: (jnp.minimum(i, np_tiles - 1), 0)),
                     pl.BlockSpec((tc, d), lambda i, p1, p2: (jnp.maximum(i - np_tiles, 0), 0)))
        scratch.append(pltpu.VMEM((tc, d), F32))
    else:
        in_specs.append(mod_spec)
        args.append(mods_next)
        out_shape = (jax.ShapeDtypeStruct((t, d), F32), jax.ShapeDtypeStruct((t, d), BF16))
        out_specs = (tile, tile)
    in_specs += [full(g2.shape), full(b2.shape)]
    args += [g2, b2]
    return pl.pallas_call(
        body,
        out_shape=out_shape,
        grid_spec=pltpu.PrefetchScalarGridSpec(
            num_scalar_prefetch=2, grid=(t // tc,),
            in_specs=in_specs, out_specs=out_specs, scratch_shapes=scratch),
        compiler_params=_cparams("arbitrary"),
        name="combine",
    )(*args)


def _ffn_routed(h2, logits, wg, wu, wd, x1, mods_l, mods_next, g2, b2, tp, ns, tm, ck):
    t = h2.shape[0]
    route, cnt = _route(logits, min(512, t))
    i1 = route[:, ROUTE_I1].astype(jnp.int32)
    i2 = route[:, ROUTE_I2].astype(jnp.int32)
    counts = cnt[0, :N_EXPERTS].astype(jnp.int32)
    padded = (counts + tm - 1) // tm * tm
    ends = jnp.cumsum(padded)
    starts = ends - padded
    pos1 = starts[i1] + route[:, ROUTE_R1].astype(jnp.int32)
    pos2 = starts[i2] + route[:, ROUTE_R2].astype(jnp.int32)
    nt = 2 * t // tm + N_EXPERTS
    n_used = ends[-1] // tm
    tile_start = jnp.arange(nt, dtype=jnp.int32) * tm
    tile_expert = jnp.sum(tile_start[:, None] >= ends[None, :], axis=1).astype(jnp.int32)
    tile_expert = jnp.minimum(tile_expert, tile_expert[n_used - 1])
    src = _row_sources(pos1, pos2, starts + counts, ends, nt * tm)
    ys = _experts(tile_expert, n_used.reshape(1), src, h2, wg, wu, wd, tm, ck)
    return _combine(pos1, pos2, ys, route, x1, mods_l, mods_next, g2, b2, tp, ns, tm)


def _permute_in_cols(w):
    d = w.shape[1]
    qk = w[..., :4 * A_HEADS * A_QK].reshape(w.shape[0], d, 2, 2, A_HEADS, A_QK)
    qk = qk.transpose(0, 1, 2, 4, 3, 5).reshape(w.shape[0], d, 4 * A_HEADS * A_QK)
    return jnp.concatenate([qk, w[..., 4 * A_HEADS * A_QK:]], axis=-1)


def kernel(x_prompt, x_sample, cache_a_k, cache_a_v, cache_c_k, cache_c_v, c, c_ctx, w_ada, b_ada, w_in, lam, a_subln, w_fnet, c_sink, d_norm_g, d_norm_b, d_ws, d_bs, w_out, ln1_g, ln1_b, ln2_g, ln2_b, ffn_wg, ffn_wu, ffn_wd, w_router, b_router, moe_wg, moe_wu, moe_wd):
    nb_p, n_p, d = x_prompt.shape
    nb_s, n_s, _ = x_sample.shape
    depth = w_in.shape[0]
    past = cache_a_k.shape[2]
    tp = nb_p * n_p
    t = tp + nb_s * n_s
    assert tp % n_s == 0 and n_s % 512 == 0 and n_p % CHUNK == 0 and nb_s + 1 <= 8

    assert depth % 2 == 0
    c_all = jnp.zeros((8, d), F32).at[0].set(c_ctx).at[1:1 + nb_s].set(c)
    mods = _modulation(c_all, w_ada, b_ada).reshape(depth, 8, 6, d)

    w_in_b = _permute_in_cols(w_in.astype(BF16))
    w_out_b = w_out.astype(BF16)
    w_fnet_b = w_fnet.astype(BF16)
    d_ws_b = d_ws.astype(BF16)
    ffn_wg_b, ffn_wu_b, ffn_wd_b = (w.astype(BF16) for w in (ffn_wg, ffn_wu, ffn_wd))
    rope_a = _rope_tables(n_s, A_QK)
    rope_c = _rope_tables(n_s, C_HD)
    csh, csl = _split(jnp.asarray(_dft_channel_table()))
    w2p_h = jnp.asarray(_dft_table(n_p)).astype(BF16)
    w2s_h = jnp.asarray(_dft_table(n_s)).astype(BF16)
    ca_k = cache_a_k.reshape(nb_s, depth, past, A_HEADS * A_V)
    ca_v = cache_a_v.reshape(nb_s, depth, past, A_HEADS * A_V)
    cc_k = cache_c_k.reshape(nb_s, depth, past, C_KV_HEADS * C_HD)
    cc_v = cache_c_v.reshape(nb_s, depth, past, C_KV_HEADS * C_HD)

    tm_big = math.gcd(1024, tp, n_s)
    tm_mid = math.gcd(512, tp, n_s)
    tm_small = math.gcd(256, tp, n_s)
    caches = tuple(jnp.zeros((nb_p, depth, n_p, cut[2]), F32) for cut in CACHE_CUTS)
    x, h = _first_ln(x_prompt.reshape(tp, d), x_sample.reshape(nb_s * n_s, d), mods[0], n_s, tm_mid)
    for l in range(depth):
        lam_init = 0.8 - 0.6 * math.exp(-0.3 * l)
        mods_next = mods[l + 1] if l + 1 < depth else None
        proj, *caches = _in_proj(h, w_in_b, l, caches, tp, tm_big)

        subln = a_subln[l].reshape(1, A_V)
        abc_p = _prompt_mixers(proj, lam[l], subln, c_sink[l], w2p_h, csh, csl,
                               w_fnet_b[l], lam_init, nb_p, n_p)
        oa_s = _sample_diff_attn(proj, ca_k, ca_v, l, rope_a[0], rope_a[1], lam[l], subln,
                                 lam_init, tp, nb_s, n_s, tm_big)
        oc_s = _sample_window_attn(proj, cc_k, cc_v, l, rope_c[0], rope_c[1], c_sink[l],
                                   tp, nb_s, n_s)
        ob_s = _sample_fourier(proj, w2s_h, csh, csl, w_fnet_b[l], tp, nb_s, n_s, tm_mid)
        abc_s = (oa_s, ob_s, oc_s)
        od = _spatial_gate(proj, d_norm_g[l].reshape(1, -1), d_norm_b[l].reshape(1, -1),
                           d_ws_b[l], d_bs[l].T, tm_mid)

        g1, b1 = ln1_g[l].reshape(1, d), ln1_b[l].reshape(1, d)
        g2, b2 = ln2_g[l].reshape(1, d), ln2_b[l].reshape(1, d)
        j = l // 2
        if l % 2 == 0:
            x1, h2 = _out_proj(abc_p, abc_s, od, w_out_b, l, x, mods[l], g1, b1, None,
                               tp, n_s, tm_mid)
            a, *moe_b = _ffn_up(h2, ffn_wg_b, ffn_wu_b, j, (moe_wg, moe_wu, moe_wd), j, tm_big, 512)
            x, h = _ffn_down(a, ffn_wd_b, j, x1, mods[l], mods_next, g2, b2, tp, n_s, tm_small)
        else:
            router = _pack_router(w_router[j], b_router[j])
            x1, h2, logits = _out_proj(abc_p, abc_s, od, w_out_b, l, x, mods[l], g1, b1, router,
                                       tp, n_s, tm_mid)
            x, h = _ffn_routed(h2, logits, *moe_b, x1, mods[l], mods_next,
                               g2, b2, tp, n_s, tm_small, 256)

    y_p = x.reshape(nb_p, n_p, d)
    y_s = h.reshape(nb_s, n_s, d)
    ak, av, ck, cv = caches
    return (y_p, y_s,
            ak.reshape(nb_p, depth, n_p, A_HEADS, 2 * A_QK), av.reshape(nb_p, depth, n_p, A_HEADS, A_V),
            ck.reshape(nb_p, depth, n_p, C_KV_HEADS, C_HD), cv.reshape(nb_p, depth, n_p, C_KV_HEADS, C_HD))
```
